```python
import jax, jax.numpy as jnp
from jax import lax
import numpy as np

D_MODEL = 1024
BATCH = 2
SEQ = 8192
DEPTH = 2
DEC_BATCH = 128
DEC_SEQ = 4
PAST_LEN = 2048
PAGE_SIZE = 128

HEAD_DIM = 64
N_HEADS_A = D_MODEL // 128
N_HEADS_B = D_MODEL // 128
N_HEADS_KV = N_HEADS_A + N_HEADS_B
WIDTH_A = N_HEADS_A * HEAD_DIM
WIDTH_B = N_HEADS_B * HEAD_DIM
QKV_WIDTH = N_HEADS_KV * HEAD_DIM
POOL_WINDOWS = (2, 4, 8, 16)
POOL_GROUPS = len(POOL_WINDOWS)
WIDTH_C = D_MODEL // 2
POOL_GROUP_DIM = WIDTH_C // POOL_GROUPS
POOL_STATE = max(POOL_WINDOWS) - 1
MOBA_BLOCK = 256
MOBA_TOPK = 3
MOBA_Q_CHUNK = 64
SB_Q_CHUNK = 128
N_BRANCHES = 3
GATE_WIDTH = WIDTH_A + WIDTH_B + WIDTH_C
IN_WIDTH = 3 * QKV_WIDTH + WIDTH_C + GATE_WIDTH + N_BRANCHES * D_MODEL
SPLIT_POINTS = [QKV_WIDTH, 2 * QKV_WIDTH, 3 * QKV_WIDTH, 3 * QKV_WIDTH + WIDTH_C,
                3 * QKV_WIDTH + WIDTH_C + GATE_WIDTH]
DEEPNORM_ALPHA = (2 * DEPTH) ** 0.25
DEEPNORM_BETA = (8 * DEPTH) ** -0.25
LN_EPS = 1e-5

kernel_name = 'hybrid_moba_stickbreak_pool_decoder'


def _chunk(n, c):
    return c if n % c == 0 else n


def _layer_norm(x, gain, bias):
    xf = x.astype(jnp.float32)
    mu = jnp.mean(xf, axis=-1, keepdims=True)
    var = jnp.mean(jnp.square(xf - mu), axis=-1, keepdims=True)
    return ((xf - mu) * lax.rsqrt(var + LN_EPS) * gain + bias).astype(x.dtype)


def _moba_sequence(q, k, v, q_start):
    t_len, n_h, dh = q.shape
    l_len = k.shape[0]
    nb = -(-l_len // MOBA_BLOCK)
    pad = nb * MOBA_BLOCK - l_len
    kb = jnp.pad(k, ((0, pad), (0, 0), (0, 0))).reshape(nb, MOBA_BLOCK, n_h, dh)
    vb = jnp.pad(v, ((0, pad), (0, 0), (0, 0))).reshape(nb, MOBA_BLOCK, n_h, dh)
    k_mean = jnp.mean(kb.astype(jnp.float32), axis=1)
    kb = kb.transpose(2, 0, 1, 3)
    vb = vb.transpose(2, 0, 1, 3)
    topk = min(MOBA_TOPK, nb)
    qc = _chunk(t_len, MOBA_Q_CHUNK)
    q_chunks = q.reshape(t_len // qc, qc, n_h, dh)
    pos_chunks = (q_start + jnp.arange(t_len, dtype=jnp.int32)).reshape(t_len // qc, qc)
    head_idx = jnp.arange(n_h)[None, :, None]
    blk_ids = jnp.arange(nb, dtype=jnp.int32)
    offs = jnp.arange(MOBA_BLOCK, dtype=jnp.int32)
    scale = HEAD_DIM ** -0.5

    def attend(args):
        qch, pos = args
        own = pos // MOBA_BLOCK
        score = jnp.einsum('thd,nhd->thn', qch.astype(jnp.float32), k_mean)
        fully_past = blk_ids[None, None, :] < own[:, None, None]
        score = jnp.where(fully_past, score, -jnp.inf)
        _, top = lax.top_k(score, topk)
        own_b = jnp.broadcast_to(own[:, None, None], (qc, n_h, 1))
        sel = jnp.concatenate([top, own_b], axis=-1)
        sel_ok = jnp.concatenate([top < own[:, None, None], jnp.ones((qc, n_h, 1), bool)], axis=-1)
        kg = kb[head_idx, sel]
        vg = vb[head_idx, sel]
        key_pos = sel[..., None] * MOBA_BLOCK + offs
        mask = sel_ok[..., None] & (key_pos <= pos[:, None, None, None])
        logits = jnp.einsum('thd,thrsd->thrs', qch, kg).astype(jnp.float32) * scale
        logits = jnp.where(mask, logits, -jnp.inf).reshape(qc, n_h, -1)
        p = jax.nn.softmax(logits, axis=-1).reshape(mask.shape).astype(v.dtype)
        return jnp.einsum('thrs,thrsd->thd', p, vg)

    out = lax.map(attend, (q_chunks, pos_chunks))
    return out.reshape(t_len, n_h, dh)


def _stick_breaking(q, k, v, q_start):
    b, t_len, n_h, dh = q.shape
    l_len = k.shape[1]
    qc = _chunk(t_len, SB_Q_CHUNK)
    nc = t_len // qc
    q_chunks = q.reshape(b, nc, qc, n_h, dh).transpose(1, 0, 2, 3, 4)
    pos_chunks = (q_start + jnp.arange(t_len, dtype=jnp.int32)).reshape(nc, qc)
    key_pos = jnp.arange(l_len, dtype=jnp.int32)
    scale = HEAD_DIM ** -0.5

    def attend(args):
        qch, pos = args
        z = jnp.einsum('bthd,bshd->bhts', qch, k).astype(jnp.float32) * scale
        causal = key_pos[None, :] < pos[:, None]
        log_stay = jnp.where(causal, jax.nn.log_sigmoid(-z), 0.0)
        tail = lax.cumsum(log_stay, axis=3, reverse=True)
        tail = jnp.concatenate([tail[..., 1:], jnp.zeros_like(tail[..., :1])], axis=-1)
        weight = jnp.where(causal, jnp.exp(jax.nn.log_sigmoid(z) + tail), 0.0)
        return jnp.einsum('bhts,bshd->bthd', weight.astype(v.dtype), v)

    out = lax.map(attend, (q_chunks, pos_chunks))
    return out.transpose(1, 0, 2, 3, 4).reshape(b, t_len, n_h, dh)


def _pool_mixer(u, state, n_past, w_pool, pool_scale):
    b, t_len, c = u.shape
    ext = jnp.concatenate([state, u], axis=1)
    csum = jnp.pad(jnp.cumsum(ext.astype(jnp.float32), axis=1), ((0, 0), (1, 0), (0, 0)))
    end = csum[:, POOL_STATE + 1:]
    t = jnp.arange(t_len, dtype=jnp.int32)
    pooled = []
    for g, w in enumerate(POOL_WINDOWS):
        cs = slice(g * POOL_GROUP_DIM, (g + 1) * POOL_GROUP_DIM)
        start = csum[:, POOL_STATE + 1 - w: POOL_STATE + 1 - w + t_len, cs]
        count = jnp.minimum(w, t + 1 + n_past).astype(jnp.float32)
        pooled.append((end[..., cs] - start) / count[None, :, None])
    diff = (jnp.concatenate(pooled, axis=-1) - u.astype(jnp.float32)).astype(u.dtype)
    mixed = jnp.einsum('btgc,gce->btge', diff.reshape(b, t_len, POOL_GROUPS, POOL_GROUP_DIM), w_pool)
    return mixed.reshape(b, t_len, c) * pool_scale, ext[:, -POOL_STATE:]


def _mixer_layer(x, past_k, past_v, pool_state, q_start, n_pool_past,
                 w_in, w_out_a, w_out_b, w_out_c, w_pool, pool_scale, w_o, ln_gain, ln_bias):
    b, t_len, _ = x.shape
    proj = jnp.einsum('btd,de->bte', x, w_in)
    q, k, v, u, z, m = jnp.split(proj, SPLIT_POINTS, axis=-1)
    q = q.reshape(b, t_len, N_HEADS_KV, HEAD_DIM)
    k = k.reshape(b, t_len, N_HEADS_KV, HEAD_DIM)
    v = v.reshape(b, t_len, N_HEADS_KV, HEAD_DIM)
    k_all = jnp.concatenate([past_k, k], axis=1)
    v_all = jnp.concatenate([past_v, v], axis=1)
    ha = N_HEADS_A
    o_a = lax.map(lambda a: _moba_sequence(a[0], a[1], a[2], q_start),
                  (q[:, :, :ha], k_all[:, :, :ha], v_all[:, :, :ha]))
    o_b = _stick_breaking(q[:, :, ha:], k_all[:, :, ha:], v_all[:, :, ha:], q_start)
    o_c, new_pool = _pool_mixer(u, pool_state, n_pool_past, w_pool, pool_scale)
    z_a, z_b, z_c = jnp.split(z, [WIDTH_A, WIDTH_A + WIDTH_B], axis=-1)
    y_a = jnp.einsum('bte,ed->btd', o_a.reshape(b, t_len, WIDTH_A) * jax.nn.silu(z_a), w_out_a)
    y_b = jnp.einsum('bte,ed->btd', o_b.reshape(b, t_len, WIDTH_B) * jax.nn.silu(z_b), w_out_b)
    y_c = jnp.einsum('bte,ed->btd', o_c * jax.nn.silu(z_c), w_out_c)
    g_a, g_b, g_c = jnp.split(jax.nn.sigmoid(m), N_BRANCHES, axis=-1)
    s = jnp.einsum('btd,de->bte', g_a * y_a + g_b * y_b + g_c * y_c, w_o)
    y = _layer_norm(DEEPNORM_ALPHA * x + s, ln_gain, ln_bias)
    return y, k, v, new_pool


def setup_inputs(seed: int = 0) -> dict:
    key = jax.random.key(seed)
    ks = jax.random.split(key, 16)
    f32 = jnp.float32
    n_pages = PAST_LEN // PAGE_SIZE
    n_used = DEC_BATCH * n_pages
    n_pool = n_used + n_used // 4
    x_prompt = jax.random.normal(ks[0], (BATCH, SEQ, D_MODEL), f32)
    x_sample = jax.random.normal(ks[1], (DEC_BATCH, DEC_SEQ, D_MODEL), f32)
    cache_k = jax.random.normal(ks[2], (DEPTH, n_pool, PAGE_SIZE, N_HEADS_KV, HEAD_DIM), f32)
    cache_v = jax.random.normal(ks[3], (DEPTH, n_pool, PAGE_SIZE, N_HEADS_KV, HEAD_DIM), f32)
    state_pool = jax.random.normal(ks[4], (DEPTH, DEC_BATCH, POOL_STATE, WIDTH_C), f32)
    page_table = jax.random.permutation(ks[5], n_pool)[:n_used].reshape(DEC_BATCH, n_pages).astype(jnp.int32)
    w_in = jax.random.normal(ks[6], (DEPTH, D_MODEL, IN_WIDTH), f32) * D_MODEL ** -0.5
    w_out_a = jax.random.normal(ks[7], (DEPTH, WIDTH_A, D_MODEL), f32) * (WIDTH_A ** -0.5 * DEEPNORM_BETA)
    w_out_b = jax.random.normal(ks[8], (DEPTH, WIDTH_B, D_MODEL), f32) * (WIDTH_B ** -0.5 * DEEPNORM_BETA)
    w_out_c = jax.random.normal(ks[9], (DEPTH, WIDTH_C, D_MODEL), f32) * (WIDTH_C ** -0.5 * DEEPNORM_BETA)
    w_pool = jax.random.normal(ks[10], (DEPTH, POOL_GROUPS, POOL_GROUP_DIM, POOL_GROUP_DIM), f32) * POOL_GROUP_DIM ** -0.5
    pool_scale = 1.0 + 0.02 * jax.random.normal(ks[11], (DEPTH, WIDTH_C), f32)
    w_o = jax.random.normal(ks[12], (DEPTH, D_MODEL, D_MODEL), f32) * (D_MODEL ** -0.5 * DEEPNORM_BETA)
    ln_gain = 1.0 + 0.02 * jax.random.normal(ks[13], (DEPTH, D_MODEL), f32)
    ln_bias = 0.02 * jax.random.normal(ks[14], (DEPTH, D_MODEL), f32)
    return {'x_prompt': x_prompt, 'x_sample': x_sample, 'cache_k': cache_k, 'cache_v': cache_v,
            'state_pool': state_pool, 'page_table': page_table, 'w_in': w_in, 'w_out_a': w_out_a,
            'w_out_b': w_out_b, 'w_out_c': w_out_c, 'w_pool': w_pool, 'pool_scale': pool_scale,
            'w_o': w_o, 'ln_gain': ln_gain, 'ln_bias': ln_bias}


def reference(x_prompt, x_sample, cache_k, cache_v, state_pool, page_table, w_in, w_out_a, w_out_b,
              w_out_c, w_pool, pool_scale, w_o, ln_gain, ln_bias):
    b_p, seq = x_prompt.shape[0], x_prompt.shape[1]
    b_s = x_sample.shape[0]
    past_len = page_table.shape[1] * cache_k.shape[2]
    empty_kv = jnp.zeros((b_p, 0, N_HEADS_KV, HEAD_DIM), x_prompt.dtype)
    pool_zero = jnp.zeros((b_p, POOL_STATE, WIDTH_C), x_prompt.dtype)
    xp, xs = x_prompt, x_sample
    kp_l, vp_l, sp_l, ks_l, vs_l, ss_l = [], [], [], [], [], []
    for l in range(DEPTH):
        wts = (w_in[l], w_out_a[l], w_out_b[l], w_out_c[l], w_pool[l], pool_scale[l], w_o[l],
               ln_gain[l], ln_bias[l])
        xp, kp, vp, sp = _mixer_layer(xp, empty_kv, empty_kv, pool_zero, 0, 0, *wts)
        past_k = cache_k[l][page_table].reshape(b_s, past_len, N_HEADS_KV, HEAD_DIM)
        past_v = cache_v[l][page_table].reshape(b_s, past_len, N_HEADS_KV, HEAD_DIM)
        xs, k_new, v_new, s_new = _mixer_layer(xs, past_k, past_v, state_pool[l], past_len,
                                               min(POOL_STATE, past_len), *wts)
        kp_l.append(kp); vp_l.append(vp); sp_l.append(sp)
        ks_l.append(k_new); vs_l.append(v_new); ss_l.append(s_new)
    page_shape = (DEPTH, b_p, seq // PAGE_SIZE, PAGE_SIZE, N_HEADS_KV, HEAD_DIM)
    new_k_prompt = jnp.stack(kp_l).reshape(page_shape)
    new_v_prompt = jnp.stack(vp_l).reshape(page_shape)
    new_pool_prompt = jnp.stack(sp_l)
    new_k_sample = jnp.stack(ks_l)
    new_v_sample = jnp.stack(vs_l)
    new_pool_sample = jnp.stack(ss_l)
    return (xp, xs, new_k_prompt, new_v_prompt, new_pool_prompt, new_k_sample, new_v_sample, new_pool_sample)
```

```python
import functools

import jax
import jax.numpy as jnp
from jax import lax
from jax.experimental import pallas as pl
from jax.experimental.pallas import tpu as pltpu

F32 = jnp.float32
BF16 = jnp.bfloat16

HEAD_DIM = 64
HEADS_PER_BRANCH = 8
BRANCH_WIDTH = HEADS_PER_BRANCH * HEAD_DIM
LANES = 128
MOBA_BLOCK = 256
MOBA_TOPK = 3
POOL_WINDOWS = (2, 4, 8, 16)
POOL_STATE = max(POOL_WINDOWS) - 1
POOL_HALO = 16
POOL_GROUP_DIM = BRANCH_WIDTH // len(POOL_WINDOWS)
LN_EPS = 1e-5
MASKED = -1e30
ATTN_SCALE = HEAD_DIM ** -0.5
VMEM_LIMIT = 48 * 1024 * 1024

Q_COL, K_COL, V_COL, U_COL, Z_COL, M_COL = 0, 2, 4, 6, 7, 10


def _dot(a, b):
    return jnp.dot(a, b, preferred_element_type=F32)


def _dot_nt(a, b):
    return lax.dot_general(a, b, (((1,), (1,)), ((), ())), preferred_element_type=F32)


def _sigmoid(x):
    return 1.0 / (1.0 + jnp.exp(-x))


def _params(*sem):
    return pltpu.CompilerParams(dimension_semantics=sem, vmem_limit_bytes=VMEM_LIMIT)


def _inproj_kernel(x_ref, w_ref, o_ref):
    o_ref[...] = _dot(x_ref[...].astype(BF16), w_ref[...])


def _inproj(x, w, tm, tn):
    m, k = x.shape
    n = w.shape[1]
    return pl.pallas_call(
        _inproj_kernel,
        grid=(n // tn, m // tm),
        in_specs=[pl.BlockSpec((tm, k), lambda j, i: (i, 0)),
                  pl.BlockSpec((k, tn), lambda j, i: (0, j))],
        out_specs=pl.BlockSpec((tm, tn), lambda j, i: (i, j)),
        out_shape=jax.ShapeDtypeStruct((m, n), F32),
        compiler_params=_params("arbitrary", "arbitrary"),
        name="inproj",
    )(x, w)


def _stick_break_tile(z, carry, strict_upper, mask):
    log_beta = jnp.minimum(z, 0.0) - jnp.log1p(jnp.exp(-jnp.abs(z)))
    log_stay = log_beta - z
    if mask is not None:
        log_stay = jnp.where(mask, log_stay, 0.0)
    hi = log_stay.astype(BF16)
    lo = (log_stay - hi.astype(F32)).astype(BF16)
    tail = _dot(hi, strict_upper) + _dot(lo, strict_upper)
    w = jnp.exp(log_beta + tail + carry)
    if mask is not None:
        w = jnp.where(mask, w, 0.0)
    return w, carry + tail[:, 0:1] + log_stay[:, 0:1]


def _strict_upper(n):
    j = lax.broadcasted_iota(jnp.int32, (n, n), 0)
    s = lax.broadcasted_iota(jnp.int32, (n, n), 1)
    return (j > s).astype(BF16)


def _moba_kernel(q_ref, k_ref, v_ref, o_ref, kaug_ref, vb_ref, kmean_ref, *, nblk):
    blk = MOBA_BLOCK
    i = pl.program_id(2)

    @pl.when(i == 0)
    def _prepare():
        kmean_ref[...] = jnp.zeros_like(kmean_ref)

        def body(j, c):
            r = pl.multiple_of(j * blk, blk)
            kb = k_ref[0, pl.ds(r, blk), :]
            vb_ref[pl.ds(r, blk), :] = v_ref[0, pl.ds(r, blk), :].astype(BF16)
            kmean_ref[pl.ds(j, 1), :] = jnp.sum(kb, axis=0, keepdims=True) / blk
            onehot = (lax.broadcasted_iota(jnp.int32, (blk, HEAD_DIM), 1) == j).astype(F32)
            for h in range(2):
                kh = kb[:, h * HEAD_DIM:(h + 1) * HEAD_DIM]
                kaug_ref[h, pl.ds(r, blk), :] = jnp.concatenate([kh, onehot], axis=1).astype(BF16)
            return c

        lax.fori_loop(0, nblk, body, 0)

    q2 = q_ref[0]
    lane = lax.broadcasted_iota(jnp.int32, (blk, HEAD_DIM), 1)
    lane_f = lane.astype(F32)
    row = lax.broadcasted_iota(jnp.int32, (blk, blk), 0)
    col = lax.broadcasted_iota(jnp.int32, (blk, blk), 1)
    causal = col <= row
    r_own = pl.multiple_of(i * blk, blk)
    outs = []
    for h in range(2):
        qh = q2[:, h * HEAD_DIM:(h + 1) * HEAD_DIM]
        km = kmean_ref[:, h * HEAD_DIM:(h + 1) * HEAD_DIM]
        score = lax.dot_general(qh, km, (((1,), (1,)), ((), ())), precision=lax.Precision.HIGHEST,
                                preferred_element_type=F32)
        sc = jnp.where(lane < i, score, -jnp.inf)
        selected = lane == i
        for _ in range(MOBA_TOPK):
            mx = jnp.max(sc, axis=1, keepdims=True)
            is_max = (sc == mx) & (sc > -jnp.inf)
            first = jnp.min(jnp.where(is_max, lane_f, float(HEAD_DIM)), axis=1, keepdims=True)
            pick = lane_f == first
            selected = selected | pick
            sc = jnp.where(pick, -jnp.inf, sc)
        bias = jnp.where(selected, 0.0, MASKED)
        qaug = jnp.concatenate([qh * ATTN_SCALE, bias], axis=1).astype(BF16)

        s = _dot_nt(qaug, kaug_ref[h, pl.ds(r_own, blk), :])
        s = jnp.where(causal, s, MASKED)
        m = jnp.max(s, axis=1, keepdims=True)
        p = jnp.exp(s - m)
        l = jnp.sum(p, axis=1, keepdims=True)
        acc = _dot(p.astype(BF16), vb_ref[pl.ds(r_own, blk), :])

        def body(j, carry):
            m, l, acc = carry
            r = pl.multiple_of(j * blk, blk)
            s = _dot_nt(qaug, kaug_ref[h, pl.ds(r, blk), :])
            m_new = jnp.maximum(m, jnp.max(s, axis=1, keepdims=True))
            alpha = jnp.exp(m - m_new)
            p = jnp.exp(s - m_new)
            l = alpha * l + jnp.sum(p, axis=1, keepdims=True)
            acc = alpha * acc + _dot(p.astype(BF16), vb_ref[pl.ds(r, blk), :])
            return m_new, l, acc

        m, l, acc = lax.fori_loop(0, i, body, (m, l, acc))
        outs.append(acc[:, h * HEAD_DIM:(h + 1) * HEAD_DIM] / l)
    o_ref[0] = jnp.concatenate(outs, axis=1)


def _moba_prompt(proj):
    b, t, _ = proj.shape
    blk = MOBA_BLOCK
    nblk = t // blk
    assert t % blk == 0 and nblk <= HEAD_DIM
    npair = BRANCH_WIDTH // LANES
    col0 = BRANCH_WIDTH // LANES
    return pl.pallas_call(
        functools.partial(_moba_kernel, nblk=nblk),
        grid=(b, npair, nblk),
        in_specs=[pl.BlockSpec((1, blk, LANES), lambda bi, hp, i: (bi, i, Q_COL * col0 + hp)),
                  pl.BlockSpec((1, t, LANES), lambda bi, hp, i: (bi, 0, K_COL * col0 + hp)),
                  pl.BlockSpec((1, t, LANES), lambda bi, hp, i: (bi, 0, V_COL * col0 + hp))],
        out_specs=pl.BlockSpec((1, blk, LANES), lambda bi, hp, i: (bi, i, hp)),
        out_shape=jax.ShapeDtypeStruct((b, t, BRANCH_WIDTH), F32),
        scratch_shapes=[pltpu.VMEM((2, t, LANES), BF16), pltpu.VMEM((t, LANES), BF16),
                        pltpu.VMEM((HEAD_DIM, LANES), F32)],
        compiler_params=_params("arbitrary", "arbitrary", "arbitrary"),
        name="moba_prompt",
    )(proj, proj, proj)


def _sb_kernel(q_ref, k_ref, v_ref, o_ref, kb_ref, vb_ref, *, nblk, blk):
    i = pl.program_id(2)

    @pl.when(i == 0)
    def _prepare():
        def body(j, c):
            r = pl.multiple_of(j * blk, blk)
            kb_ref[pl.ds(r, blk), :] = k_ref[0, pl.ds(r, blk), :].astype(BF16)
            vb_ref[pl.ds(r, blk), :] = v_ref[0, pl.ds(r, blk), :].astype(BF16)
            return c

        lax.fori_loop(0, nblk, body, 0)

    q2 = q_ref[0] * ATTN_SCALE
    lane = lax.broadcasted_iota(jnp.int32, (blk, LANES), 1)
    row = lax.broadcasted_iota(jnp.int32, (blk, blk), 0)
    col = lax.broadcasted_iota(jnp.int32, (blk, blk), 1)
    strictly_past = col < row
    upper = _strict_upper(blk)
    r_own = pl.multiple_of(i * blk, blk)
    outs = []
    for h in range(2):
        qh = jnp.where(lane // HEAD_DIM == h, q2, 0.0).astype(BF16)

        z = _dot_nt(qh, kb_ref[pl.ds(r_own, blk), :])
        w, carry = _stick_break_tile(z, jnp.zeros((blk, 1), F32), upper, strictly_past)
        acc = _dot(w.astype(BF16), vb_ref[pl.ds(r_own, blk), :])

        def body(step, state):
            carry, acc = state
            r = pl.multiple_of((i - 1 - step) * blk, blk)
            z = _dot_nt(qh, kb_ref[pl.ds(r, blk), :])
            w, carry = _stick_break_tile(z, carry, upper, None)
            return carry, acc + _dot(w.astype(BF16), vb_ref[pl.ds(r, blk), :])

        _, acc = lax.fori_loop(0, i, body, (carry, acc))
        outs.append(acc[:, h * HEAD_DIM:(h + 1) * HEAD_DIM])
    o_ref[0] = jnp.concatenate(outs, axis=1)


def _sb_prompt(proj, blk):
    b, t, _ = proj.shape
    nblk = t // blk
    assert t % blk == 0
    npair = BRANCH_WIDTH // LANES
    col0 = BRANCH_WIDTH // LANES
    return pl.pallas_call(
        functools.partial(_sb_kernel, nblk=nblk, blk=blk),
        grid=(b, npair, nblk),
        in_specs=[pl.BlockSpec((1, blk, LANES), lambda bi, hp, i: (bi, i, (Q_COL + 1) * col0 + hp)),
                  pl.BlockSpec((1, t, LANES), lambda bi, hp, i: (bi, 0, (K_COL + 1) * col0 + hp)),
                  pl.BlockSpec((1, t, LANES), lambda bi, hp, i: (bi, 0, (V_COL + 1) * col0 + hp))],
        out_specs=pl.BlockSpec((1, blk, LANES), lambda bi, hp, i: (bi, i, hp)),
        out_shape=jax.ShapeDtypeStruct((b, t, BRANCH_WIDTH), F32),
        scratch_shapes=[pltpu.VMEM((t, LANES), BF16), pltpu.VMEM((t, LANES), BF16)],
        compiler_params=_params("arbitrary", "arbitrary", "arbitrary"),
        name="sb_prompt",
    )(proj, proj, proj)


def _pool_prompt_kernel(u_ref, halo_ref, o_ref, ext_ref, *, tm):
    i = pl.program_id(1)
    ext_ref[0:POOL_HALO, :] = jnp.where(i == 0, 0.0, halo_ref[0])
    ext_ref[POOL_HALO:POOL_HALO + tm, :] = u_ref[0]
    t = i * tm + lax.broadcasted_iota(jnp.int32, (tm, 1), 0)
    for g, w in enumerate(POOL_WINDOWS):
        cs = slice(g * POOL_GROUP_DIM, (g + 1) * POOL_GROUP_DIM)
        u = ext_ref[POOL_HALO:POOL_HALO + tm, cs]
        total = u
        for k in range(1, w):
            total = total + ext_ref[POOL_HALO - k:POOL_HALO - k + tm, cs]
        count = jnp.minimum(w, t + 1).astype(F32)
        o_ref[0, :, cs] = total / count - u


def _pool_prompt(proj, tm):
    b, t, _ = proj.shape
    assert t % tm == 0 and tm % POOL_HALO == 0
    halo_blocks = tm // POOL_HALO
    return pl.pallas_call(
        functools.partial(_pool_prompt_kernel, tm=tm),
        grid=(b, t // tm),
        in_specs=[pl.BlockSpec((1, tm, BRANCH_WIDTH), lambda bi, i: (bi, i, U_COL)),
                  pl.BlockSpec((1, POOL_HALO, BRANCH_WIDTH),
                               lambda bi, i: (bi, jnp.maximum(i * halo_blocks - 1, 0), U_COL))],
        out_specs=pl.BlockSpec((1, tm, BRANCH_WIDTH), lambda bi, i: (bi, i, 0)),
        out_shape=jax.ShapeDtypeStruct((b, t, BRANCH_WIDTH), F32),
        scratch_shapes=[pltpu.VMEM((POOL_HALO + tm, BRANCH_WIDTH), F32)],
        compiler_params=_params("arbitrary", "arbitrary"),
        name="pool_prompt",
    )(proj, proj)


def _pool_sample_kernel(state_ref, u_ref, o_ref, *, n_new, n_past):
    def ext(r):
        return state_ref[r] if r < POOL_STATE else u_ref[r - POOL_STATE]

    for t in range(n_new):
        for g, w in enumerate(POOL_WINDOWS):
            cs = slice(g * POOL_GROUP_DIM, (g + 1) * POOL_GROUP_DIM)
            total = u_ref[t][:, cs]
            for k in range(1, w):
                total = total + ext(POOL_STATE + t - k)[:, cs]
            count = float(min(w, t + 1 + n_past))
            o_ref[t, :, cs] = total / count - u_ref[t][:, cs]


def _pool_sample(state_t, u_t, n_past):
    n_new, b, c = u_t.shape
    return pl.pallas_call(
        functools.partial(_pool_sample_kernel, n_new=n_new, n_past=n_past),
        out_shape=jax.ShapeDtypeStruct((n_new, b, c), F32),
        compiler_params=pltpu.CompilerParams(vmem_limit_bytes=VMEM_LIMIT),
        name="pool_sample",
    )(state_t, u_t)


def _merge_kernel(oa_ref, ob_ref, df_ref, za_ref, zb_ref, zc_ref, ma_ref, mb_ref, mc_ref, x_ref,
                  woa_ref, wob_ref, woc_ref, wp_ref, ps_ref, wo_ref, gain_ref, bias_ref, y_ref, *, alpha):
    def silu(z):
        return z * _sigmoid(z)

    diff = df_ref[...]
    mixed = jnp.concatenate(
        [_dot(diff[:, g * POOL_GROUP_DIM:(g + 1) * POOL_GROUP_DIM].astype(BF16), wp_ref[g])
         for g in range(len(POOL_WINDOWS))], axis=1)
    o_c = mixed * ps_ref[...]
    y_a = _dot((oa_ref[...] * silu(za_ref[...])).astype(BF16), woa_ref[...])
    y_b = _dot((ob_ref[...] * silu(zb_ref[...])).astype(BF16), wob_ref[...])
    y_c = _dot((o_c * silu(zc_ref[...])).astype(BF16), woc_ref[...])
    merged = _sigmoid(ma_ref[...]) * y_a + _sigmoid(mb_ref[...]) * y_b + _sigmoid(mc_ref[...]) * y_c
    r = alpha * x_ref[...] + _dot(merged.astype(BF16), wo_ref[...])
    mu = jnp.mean(r, axis=1, keepdims=True)
    d = r - mu
    var = jnp.mean(d * d, axis=1, keepdims=True)
    y_ref[...] = d * lax.rsqrt(var + LN_EPS) * gain_ref[...] + bias_ref[...]


def _merge(o_a, o_b, diff, proj, x, wts, alpha, tm):
    rows, d_model = x.shape
    assert rows % tm == 0
    c = BRANCH_WIDTH
    gates_col0 = M_COL * c // d_model
    assert gates_col0 * d_model == M_COL * c

    def rowblk(width, colblk):
        return pl.BlockSpec((tm, width), lambda i: (i, colblk))

    def whole(a):
        return pl.BlockSpec(a.shape, lambda i: (0,) * a.ndim)

    w_out_a, w_out_b, w_out_c, w_pool, pool_scale, w_o, gain, bias = wts
    return pl.pallas_call(
        functools.partial(_merge_kernel, alpha=alpha),
        grid=(rows // tm,),
        in_specs=[rowblk(c, 0), rowblk(c, 0), rowblk(c, 0),
                  rowblk(c, Z_COL), rowblk(c, Z_COL + 1), rowblk(c, Z_COL + 2),
                  rowblk(d_model, gates_col0), rowblk(d_model, gates_col0 + 1), rowblk(d_model, gates_col0 + 2),
                  rowblk(d_model, 0)] + [whole(a) for a in wts],
        out_specs=rowblk(d_model, 0),
        out_shape=jax.ShapeDtypeStruct((rows, d_model), F32),
        compiler_params=_params("arbitrary"),
        name="merge",
    )(o_a, o_b, diff, proj, proj, proj, proj, proj, proj, x, *wts)


def _decode_kernel(pt_ref, q_ref, kn_ref, vn_ref, k0_ref, k1_ref, v0_ref, v1_ref, oa_ref, ob_ref,
                   qf_ref, qb_ref, kn_pad_ref, vn_pad_ref, m_ref, l_ref, acc_ref, score_ref,
                   carry_ref, accb_ref, *, n_new, nblk, page):
    del pt_ref
    s = pl.program_id(1)
    width = 2 * BRANCH_WIDTH
    nrow = HEADS_PER_BRANCH * n_new
    half = BRANCH_WIDTH

    def attend(k_tile, v_tile, mask_a, mask_b, slot):
        n = k_tile.shape[0]
        logits = _dot_nt(qb_ref[...], k_tile)
        sa = logits[0:nrow]
        if mask_a is not None:
            sa = jnp.where(mask_a, sa, MASKED)
        m = jnp.max(sa, axis=1, keepdims=True)
        p = jnp.exp(sa - m)
        m_ref[slot] = m
        l_ref[slot] = jnp.sum(p, axis=1, keepdims=True)
        acc_ref[slot] = _dot(p.astype(BF16), v_tile[:, 0:half])
        w, carry = _stick_break_tile(logits[nrow:2 * nrow], carry_ref[...], _strict_upper(n), mask_b)
        carry_ref[...] = carry
        accb_ref[...] += _dot(w.astype(BF16), v_tile[:, half:width])

    @pl.when(s == 0)
    def _start():
        q = q_ref[0]
        rows = [jnp.broadcast_to(q[t:t + 1, :], (HEADS_PER_BRANCH, width)) for t in range(n_new)]
        qrep = jnp.concatenate(rows + rows, axis=0)
        r = lax.broadcasted_iota(jnp.int32, (2 * nrow, width), 0)
        c = lax.broadcasted_iota(jnp.int32, (2 * nrow, width), 1)
        head = r % HEADS_PER_BRANCH + HEADS_PER_BRANCH * (r // nrow)
        qm = jnp.where(c // HEAD_DIM == head, qrep, 0.0)
        qf_ref[...] = qm
        qb_ref[...] = (qm * ATTN_SCALE).astype(BF16)
        kn_pad_ref[...] = jnp.zeros_like(kn_pad_ref)
        vn_pad_ref[...] = jnp.zeros_like(vn_pad_ref)
        kn_pad_ref[0:n_new, :] = kn_ref[0]
        vn_pad_ref[0:n_new, :] = vn_ref[0]
        carry_ref[...] = jnp.zeros_like(carry_ref)
        accb_ref[...] = jnp.zeros_like(accb_ref)
        token = lax.broadcasted_iota(jnp.int32, (nrow, page), 0) // HEADS_PER_BRANCH
        key = lax.broadcasted_iota(jnp.int32, (nrow, page), 1)
        attend(kn_pad_ref[...].astype(BF16), vn_pad_ref[...].astype(BF16), key <= token, key < token, nblk)

    blk_id = nblk - 1 - s
    k_tile = jnp.concatenate([k0_ref[0], k1_ref[0]], axis=0)
    v_tile = jnp.concatenate([v0_ref[0], v1_ref[0]], axis=0)
    k_mean = jnp.sum(k_tile[:, 0:half], axis=0, keepdims=True) / (2 * page)
    score_ref[blk_id] = jnp.sum(qf_ref[0:nrow, 0:half] * k_mean, axis=1, keepdims=True)
    attend(k_tile.astype(BF16), v_tile.astype(BF16), None, None, blk_id)

    @pl.when(s == nblk - 1)
    def _finish():
        sc = [score_ref[j] for j in range(nblk)]
        sel = [jnp.zeros((nrow, 1), jnp.bool_) for _ in range(nblk)]
        for _ in range(min(MOBA_TOPK, nblk)):
            mx = functools.reduce(jnp.maximum, sc)
            found = jnp.zeros((nrow, 1), jnp.bool_)
            for j in range(nblk):
                pick = (sc[j] == mx) & (sc[j] > -jnp.inf) & jnp.logical_not(found)
                sel[j] = sel[j] | pick
                found = found | pick
                sc[j] = jnp.where(pick, -jnp.inf, sc[j])
        m_all = m_ref[nblk]
        for j in range(nblk):
            m_all = jnp.maximum(m_all, jnp.where(sel[j], m_ref[j], MASKED))
        wgt = jnp.exp(m_ref[nblk] - m_all)
        l = wgt * l_ref[nblk]
        acc = wgt * acc_ref[nblk]
        for j in range(nblk):
            wgt = jnp.where(sel[j], jnp.exp(m_ref[j] - m_all), 0.0)
            l = l + wgt * l_ref[j]
            acc = acc + wgt * acc_ref[j]
        r = lax.broadcasted_iota(jnp.int32, (nrow, half), 0)
        c = lax.broadcasted_iota(jnp.int32, (nrow, half), 1)
        own_head = c // HEAD_DIM == r % HEADS_PER_BRANCH

        def per_token(a):
            a = jnp.where(own_head, a, 0.0).reshape(n_new, HEADS_PER_BRANCH, half)
            return jnp.sum(a, axis=1)

        oa_ref[0] = per_token(acc / l)
        ob_ref[0] = per_token(accb_ref[...])


def _decode(proj, cache_k, cache_v, page_table):
    b, n_new, _ = proj.shape
    n_pages = page_table.shape[1]
    page = cache_k.shape[1]
    width = 2 * BRANCH_WIDTH
    assert cache_k.shape[2] == width and 2 * page == MOBA_BLOCK and n_pages % 2 == 0
    assert n_new <= 8 and n_new <= page
    nblk = n_pages // 2
    nrow = HEADS_PER_BRANCH * n_new

    def new_spec(colblk):
        return pl.BlockSpec((1, n_new, width), lambda bi, s, pt: (bi, 0, colblk))

    def page_spec(which):
        return pl.BlockSpec((1, page, width),
                            lambda bi, s, pt: (pt[bi, 2 * (nblk - 1 - s) + which], 0, 0))

    out_spec = pl.BlockSpec((1, n_new, BRANCH_WIDTH), lambda bi, s, pt: (bi, 0, 0))
    grid_spec = pltpu.PrefetchScalarGridSpec(
        num_scalar_prefetch=1,
        grid=(b, nblk),
        in_specs=[new_spec(Q_COL // 2), new_spec(K_COL // 2), new_spec(V_COL // 2),
                  page_spec(0), page_spec(1), page_spec(0), page_spec(1)],
        out_specs=[out_spec, out_spec],
        scratch_shapes=[pltpu.VMEM((2 * nrow, width), F32), pltpu.VMEM((2 * nrow, width), BF16),
                        pltpu.VMEM((page, width), F32), pltpu.VMEM((page, width), F32),
                        pltpu.VMEM((nblk + 1, nrow, 1), F32), pltpu.VMEM((nblk + 1, nrow, 1), F32),
                        pltpu.VMEM((nblk + 1, nrow, BRANCH_WIDTH), F32), pltpu.VMEM((nblk, nrow, 1), F32),
                        pltpu.VMEM((nrow, 1), F32), pltpu.VMEM((nrow, BRANCH_WIDTH), F32)],
    )
    out = jax.ShapeDtypeStruct((b, n_new, BRANCH_WIDTH), F32)
    return pl.pallas_call(
        functools.partial(_decode_kernel, n_new=n_new, nblk=nblk, page=page),
        grid_spec=grid_spec,
        out_shape=[out, out],
        compiler_params=_params("arbitrary", "arbitrary"),
        name="decode",
    )(page_table, proj, proj, proj, cache_k, cache_k, cache_v, cache_v)


def _row_tile(rows, want):
    return want if rows % want == 0 else rows


def _layer(xp, xs, cache_k, cache_v, state, page_table, w_in, merge_wts, alpha, sb_block):
    b, t, d = xp.shape
    bs, ts, _ = xs.shape
    n_in = w_in.shape[1]
    c = BRANCH_WIDTH

    proj_p = _inproj(xp.reshape(b * t, d), w_in, _row_tile(b * t, 512), 1024).reshape(b, t, n_in)
    proj_s = _inproj(xs.reshape(bs * ts, d), w_in, _row_tile(bs * ts, 512), 1024).reshape(bs, ts, n_in)

    oa_p = _moba_prompt(proj_p)
    ob_p = _sb_prompt(proj_p, sb_block)
    diff_p = _pool_prompt(proj_p, _row_tile(t, 256))
    yp = _merge(oa_p.reshape(b * t, c), ob_p.reshape(b * t, c), diff_p.reshape(b * t, c),
                proj_p.reshape(b * t, n_in), xp.reshape(b * t, d), merge_wts, alpha,
                _row_tile(b * t, 256)).reshape(b, t, d)

    past_len = page_table.shape[1] * cache_k.shape[1]
    oa_s, ob_s = _decode(proj_s, cache_k, cache_v, page_table)
    u_s = proj_s[:, :, U_COL * c:(U_COL + 1) * c]
    diff_s = _pool_sample(state.transpose(1, 0, 2), u_s.transpose(1, 0, 2), min(POOL_STATE, past_len))
    diff_s = diff_s.transpose(1, 0, 2)
    ys = _merge(oa_s.reshape(bs * ts, c), ob_s.reshape(bs * ts, c), diff_s.reshape(bs * ts, c),
                proj_s.reshape(bs * ts, n_in), xs.reshape(bs * ts, d), merge_wts, alpha,
                _row_tile(bs * ts, 256)).reshape(bs, ts, d)

    kv = 2 * c
    outs = dict(
        kp=proj_p[:, :, K_COL * c:K_COL * c + kv], vp=proj_p[:, :, V_COL * c:V_COL * c + kv],
        sp=proj_p[:, t - POOL_STATE:, U_COL * c:(U_COL + 1) * c],
        ks=proj_s[:, :, K_COL * c:K_COL * c + kv], vs=proj_s[:, :, V_COL * c:V_COL * c + kv],
        ss=jnp.concatenate([state, u_s], axis=1)[:, -POOL_STATE:])
    return yp, ys, outs


def kernel(x_prompt, x_sample, cache_k, cache_v, state_pool, page_table, w_in, w_out_a, w_out_b, w_out_c,
           w_pool, pool_scale, w_o, ln_gain, ln_bias):
    depth = w_in.shape[0]
    b, t, d = x_prompt.shape
    bs, ts, _ = x_sample.shape
    n_pool, page, n_heads, head_dim = cache_k.shape[1:]
    assert head_dim == HEAD_DIM and n_heads == 2 * HEADS_PER_BRANCH and t >= POOL_STATE
    alpha = (2 * depth) ** 0.25
    xp, xs = x_prompt, x_sample
    per_layer = []
    for l in range(depth):
        merge_wts = (w_out_a[l].astype(BF16), w_out_b[l].astype(BF16), w_out_c[l].astype(BF16),
                     w_pool[l].astype(BF16), pool_scale[l].reshape(1, -1), w_o[l].astype(BF16),
                     ln_gain[l].reshape(1, -1), ln_bias[l].reshape(1, -1))
        xp, xs, outs = _layer(xp, xs, cache_k[l].reshape(n_pool, page, n_heads * head_dim),
                              cache_v[l].reshape(n_pool, page, n_heads * head_dim), state_pool[l],
                              page_table, w_in[l].astype(BF16), merge_wts, alpha, 256)
        per_layer.append(outs)

    def stack(name):
        return jnp.stack([o[name] for o in per_layer])

    page_shape = (depth, b, t // page, page, n_heads, head_dim)
    return (xp, xs,
            stack("kp").reshape(page_shape), stack("vp").reshape(page_shape), stack("sp"),
            stack("ks").reshape(depth, bs, ts, n_heads, head_dim),
            stack("vs").reshape(depth, bs, ts, n_heads, head_dim), stack("ss"))
```

```python
import functools

import jax
import jax.numpy as jnp
from jax import lax
from jax.experimental import pallas as pl
from jax.experimental.pallas import tpu as pltpu

F32 = jnp.float32
BF16 = jnp.bfloat16

HEAD_DIM = 64
HEADS_PER_BRANCH = 8
BRANCH_WIDTH = HEADS_PER_BRANCH * HEAD_DIM
LANES = 128
MOBA_BLOCK = 256
MOBA_TOPK = 3
POOL_WINDOWS = (2, 4, 8, 16)
POOL_STATE = max(POOL_WINDOWS) - 1
POOL_HALO = 16
POOL_GROUP_DIM = BRANCH_WIDTH // len(POOL_WINDOWS)
LN_EPS = 1e-5
MASKED = -1e30
ATTN_SCALE = HEAD_DIM ** -0.5
LOG2E = 1.4426950408889634
ATTN_GROUP = 4
VMEM_LIMIT = 48 * 1024 * 1024

Q_COL, K_COL, V_COL, U_COL, Z_COL, M_COL = 0, 2, 4, 6, 7, 10


def _dot(a, b):
    return jnp.dot(a, b, preferred_element_type=F32)


def _dot_nt(a, b):
    return lax.dot_general(a, b, (((1,), (1,)), ((), ())), preferred_element_type=F32)


def _sigmoid(x):
    return 1.0 / (1.0 + jnp.exp(-x))


def _params(*sem):
    return pltpu.CompilerParams(dimension_semantics=sem, vmem_limit_bytes=VMEM_LIMIT)


def _inproj_kernel(x_ref, w_ref, o_ref):
    o_ref[...] = _dot(x_ref[...].astype(BF16), w_ref[...])


def _inproj(x, w, tm, tn):
    m, k = x.shape
    n = w.shape[1]
    return pl.pallas_call(
        _inproj_kernel,
        grid=(n // tn, m // tm),
        in_specs=[pl.BlockSpec((tm, k), lambda j, i: (i, 0)),
                  pl.BlockSpec((k, tn), lambda j, i: (0, j))],
        out_specs=pl.BlockSpec((tm, tn), lambda j, i: (i, j)),
        out_shape=jax.ShapeDtypeStruct((m, n), F32),
        compiler_params=_params("arbitrary", "arbitrary"),
        name="inproj",
    )(x, w)


def _stick_break_tile(z, carry, strict_upper2, mask):
    soft = jnp.log2(1.0 + jnp.exp2(-jnp.abs(z)))
    log_beta = jnp.minimum(z, 0.0) - soft
    log_stay = log_beta - z
    if mask is not None:
        log_stay = jnp.where(mask, log_stay, 0.0)
    hi = log_stay.astype(BF16)
    lo = (log_stay - hi.astype(F32)).astype(BF16)
    tail = _dot(jnp.concatenate([hi, lo], axis=1), strict_upper2)
    w = jnp.exp2(log_beta + tail + carry)
    if mask is not None:
        w = jnp.where(mask, w, 0.0)
    return w, carry + tail[:, 0:1] + log_stay[:, 0:1]


def _strict_upper2(n):
    j = lax.broadcasted_iota(jnp.int32, (2 * n, n), 0) % n
    s = lax.broadcasted_iota(jnp.int32, (2 * n, n), 1)
    return (j > s).astype(BF16)


def _split_heads(x, lane):
    return jnp.concatenate([jnp.where(lane < HEAD_DIM, x, 0.0), jnp.where(lane >= HEAD_DIM, x, 0.0)], axis=0)


def _stage_values(v_ref, vaug_ref, ngrp, gw):
    lane = lax.broadcasted_iota(jnp.int32, (gw, LANES), 1)

    def body(g, c):
        r = pl.multiple_of(g * gw, gw)
        vaug_ref[g] = _split_heads(v_ref[0, pl.ds(r, gw), :], lane).astype(BF16)
        return c

    lax.fori_loop(0, ngrp, body, 0)


def _moba_kernel(q_ref, k_ref, v_ref, o_ref, kaug_ref, vaug_ref, kmean_ref, *, nblk, grp):
    blk = MOBA_BLOCK
    gw = grp * blk
    i = pl.program_id(2)

    @pl.when(i == 0)
    def _prepare():
        kmean_ref[...] = jnp.zeros_like(kmean_ref)
        _stage_values(v_ref, vaug_ref, nblk // grp, gw)

        def body(j, c):
            r = pl.multiple_of(j * blk, blk)
            kb = k_ref[0, pl.ds(r, blk), :]
            kmean_ref[pl.ds(j, 1), :] = jnp.sum(kb, axis=0, keepdims=True) / blk
            onehot = (lax.broadcasted_iota(jnp.int32, (blk, LANES), 1) == j).astype(F32)
            kaug_ref[pl.ds(r, blk), :] = jnp.concatenate([kb, onehot], axis=1).astype(BF16)
            return c

        lax.fori_loop(0, nblk, body, 0)

    q2 = q_ref[0]
    lane = lax.broadcasted_iota(jnp.int32, (blk, HEAD_DIM), 1)
    lane_f = lane.astype(F32)
    lane2 = lax.broadcasted_iota(jnp.int32, (blk, LANES), 1)
    biases = []
    for h in range(2):
        qh = q2[:, h * HEAD_DIM:(h + 1) * HEAD_DIM]
        km = kmean_ref[:, h * HEAD_DIM:(h + 1) * HEAD_DIM]
        score = lax.dot_general(qh, km, (((1,), (1,)), ((), ())), precision=lax.Precision.HIGHEST,
                                preferred_element_type=F32)
        sc = jnp.where(lane < i, score, -jnp.inf)
        selected = lane == i
        for _ in range(MOBA_TOPK):
            mx = jnp.max(sc, axis=1, keepdims=True)
            is_max = (sc == mx) & (sc > -jnp.inf)
            first = jnp.min(jnp.where(is_max, lane_f, float(HEAD_DIM)), axis=1, keepdims=True)
            pick = lane_f == first
            selected = selected | pick
            sc = jnp.where(pick, -jnp.inf, sc)
        biases.append(jnp.where(selected, 0.0, MASKED))
    q_log2 = _split_heads(q2 * (ATTN_SCALE * LOG2E), lane2)
    pad = jnp.zeros((2 * blk, HEAD_DIM), F32)
    qaug = jnp.concatenate([q_log2, jnp.concatenate(biases, axis=0), pad], axis=1).astype(BF16)

    def scores(g):
        r = pl.multiple_of(g * gw, gw)
        return _dot_nt(qaug, kaug_ref[pl.ds(r, gw), :])

    def weighted_values(p, g):
        both = jnp.concatenate([p[0:blk], p[blk:2 * blk]], axis=1).astype(BF16)
        return _dot(both, vaug_ref[g])

    def per_head(x):
        return jnp.where(lane2 < HEAD_DIM, x[0:blk], x[blk:2 * blk])

    g_own = i // grp
    q_pos = i * blk + lax.broadcasted_iota(jnp.int32, (2 * blk, gw), 0) % blk
    k_pos = g_own * gw + lax.broadcasted_iota(jnp.int32, (2 * blk, gw), 1)
    s = jnp.where(k_pos <= q_pos, scores(g_own), MASKED)
    m = jnp.max(s, axis=1, keepdims=True)
    p = jnp.exp2(s - m)
    l = jnp.sum(p, axis=1, keepdims=True)
    acc = weighted_values(p, g_own)

    def body(g, carry):
        m, l, acc = carry
        s = scores(g)
        m_new = jnp.maximum(m, jnp.max(s, axis=1, keepdims=True))
        alpha = jnp.exp2(m - m_new)
        p = jnp.exp2(s - m_new)
        l = alpha * l + jnp.sum(p, axis=1, keepdims=True)
        acc = per_head(alpha) * acc + weighted_values(p, g)
        return m_new, l, acc

    m, l, acc = lax.fori_loop(0, g_own, body, (m, l, acc))
    o_ref[0] = acc / per_head(l)


def _moba_prompt(proj, grp):
    b, t, _ = proj.shape
    blk = MOBA_BLOCK
    nblk = t // blk
    assert t % (grp * blk) == 0 and nblk <= HEAD_DIM
    npair = BRANCH_WIDTH // LANES
    col0 = BRANCH_WIDTH // LANES
    return pl.pallas_call(
        functools.partial(_moba_kernel, nblk=nblk, grp=grp),
        grid=(b, npair, nblk),
        in_specs=[pl.BlockSpec((1, blk, LANES), lambda bi, hp, i: (bi, i, Q_COL * col0 + hp)),
                  pl.BlockSpec((1, t, LANES), lambda bi, hp, i: (bi, 0, K_COL * col0 + hp)),
                  pl.BlockSpec((1, t, LANES), lambda bi, hp, i: (bi, 0, V_COL * col0 + hp))],
        out_specs=pl.BlockSpec((1, blk, LANES), lambda bi, hp, i: (bi, i, hp)),
        out_shape=jax.ShapeDtypeStruct((b, t, BRANCH_WIDTH), F32),
        scratch_shapes=[pltpu.VMEM((t, 2 * LANES), BF16), pltpu.VMEM((nblk // grp, 2 * grp * blk, LANES), BF16),
                        pltpu.VMEM((HEAD_DIM, LANES), F32)],
        compiler_params=_params("arbitrary", "arbitrary", "arbitrary"),
        name="moba_prompt",
    )(proj, proj, proj)


def _sb_kernel(q_ref, k_ref, v_ref, o_ref, kb_ref, vaug_ref, *, nblk, blk, grp):
    gw = grp * blk
    i = pl.program_id(2)

    @pl.when(i == 0)
    def _prepare():
        _stage_values(v_ref, vaug_ref, nblk // grp, gw)

        def body(j, c):
            r = pl.multiple_of(j * blk, blk)
            kb_ref[pl.ds(r, blk), :] = k_ref[0, pl.ds(r, blk), :].astype(BF16)
            return c

        lax.fori_loop(0, nblk, body, 0)

    lane = lax.broadcasted_iota(jnp.int32, (blk, LANES), 1)
    qs = _split_heads(q_ref[0] * (ATTN_SCALE * LOG2E), lane).astype(BF16)
    upper2 = _strict_upper2(blk)

    def tile(g, carry, q_pos):
        r = pl.multiple_of(g * gw, gw)
        z = _dot_nt(qs, kb_ref[pl.ds(r, gw), :])
        ws = [None] * grp
        for b in reversed(range(grp)):
            mask = None
            if q_pos is not None:
                k_pos = g * gw + b * blk + lax.broadcasted_iota(jnp.int32, (2 * blk, blk), 1)
                mask = k_pos < q_pos
            ws[b], carry = _stick_break_tile(z[:, b * blk:(b + 1) * blk], carry, upper2, mask)
        w = jnp.concatenate(ws, axis=1)
        both = jnp.concatenate([w[0:blk], w[blk:2 * blk]], axis=1).astype(BF16)
        return carry, _dot(both, vaug_ref[g])

    g_own = i // grp
    q_pos = i * blk + lax.broadcasted_iota(jnp.int32, (2 * blk, blk), 0) % blk
    carry, acc = tile(g_own, jnp.zeros((2 * blk, 1), F32), q_pos)

    def body(step, state):
        carry, acc = state
        carry, part = tile(g_own - 1 - step, carry, None)
        return carry, acc + part

    _, acc = lax.fori_loop(0, g_own, body, (carry, acc))
    o_ref[0] = acc


def _sb_prompt(proj, blk, grp):
    b, t, _ = proj.shape
    nblk = t // blk
    assert t % (grp * blk) == 0
    npair = BRANCH_WIDTH // LANES
    col0 = BRANCH_WIDTH // LANES
    return pl.pallas_call(
        functools.partial(_sb_kernel, nblk=nblk, blk=blk, grp=grp),
        grid=(b, npair, nblk),
        in_specs=[pl.BlockSpec((1, blk, LANES), lambda bi, hp, i: (bi, i, (Q_COL + 1) * col0 + hp)),
                  pl.BlockSpec((1, t, LANES), lambda bi, hp, i: (bi, 0, (K_COL + 1) * col0 + hp)),
                  pl.BlockSpec((1, t, LANES), lambda bi, hp, i: (bi, 0, (V_COL + 1) * col0 + hp))],
        out_specs=pl.BlockSpec((1, blk, LANES), lambda bi, hp, i: (bi, i, hp)),
        out_shape=jax.ShapeDtypeStruct((b, t, BRANCH_WIDTH), F32),
        scratch_shapes=[pltpu.VMEM((t, LANES), BF16), pltpu.VMEM((nblk // grp, 2 * grp * blk, LANES), BF16)],
        compiler_params=_params("arbitrary", "arbitrary", "arbitrary"),
        name="sb_prompt",
    )(proj, proj, proj)


def _pool_prompt_kernel(u_ref, halo_ref, o_ref, ext_ref, *, tm):
    i = pl.program_id(1)
    ext_ref[0:POOL_HALO, :] = jnp.where(i == 0, 0.0, halo_ref[0])
    ext_ref[POOL_HALO:POOL_HALO + tm, :] = u_ref[0]
    t = i * tm + lax.broadcasted_iota(jnp.int32, (tm, 1), 0)
    for g, w in enumerate(POOL_WINDOWS):
        cs = slice(g * POOL_GROUP_DIM, (g + 1) * POOL_GROUP_DIM)
        u = ext_ref[POOL_HALO:POOL_HALO + tm, cs]
        total = u
        for k in range(1, w):
            total = total + ext_ref[POOL_HALO - k:POOL_HALO - k + tm, cs]
        count = jnp.minimum(w, t + 1).astype(F32)
        o_ref[0, :, cs] = total / count - u


def _pool_prompt(proj, tm):
    b, t, _ = proj.shape
    assert t % tm == 0 and tm % POOL_HALO == 0
    halo_blocks = tm // POOL_HALO
    return pl.pallas_call(
        functools.partial(_pool_prompt_kernel, tm=tm),
        grid=(b, t // tm),
        in_specs=[pl.BlockSpec((1, tm, BRANCH_WIDTH), lambda bi, i: (bi, i, U_COL)),
                  pl.BlockSpec((1, POOL_HALO, BRANCH_WIDTH),
                               lambda bi, i: (bi, jnp.maximum(i * halo_blocks - 1, 0), U_COL))],
        out_specs=pl.BlockSpec((1, tm, BRANCH_WIDTH), lambda bi, i: (bi, i, 0)),
        out_shape=jax.ShapeDtypeStruct((b, t, BRANCH_WIDTH), F32),
        scratch_shapes=[pltpu.VMEM((POOL_HALO + tm, BRANCH_WIDTH), F32)],
        compiler_params=_params("arbitrary", "arbitrary"),
        name="pool_prompt",
    )(proj, proj)


def _pool_sample_kernel(state_ref, u_ref, o_ref, *, n_new, n_past):
    def ext(r):
        return state_ref[r] if r < POOL_STATE else u_ref[r - POOL_STATE]

    for t in range(n_new):
        for g, w in enumerate(POOL_WINDOWS):
            cs = slice(g * POOL_GROUP_DIM, (g + 1) * POOL_GROUP_DIM)
            total = u_ref[t][:, cs]
            for k in range(1, w):
                total = total + ext(POOL_STATE + t - k)[:, cs]
            count = float(min(w, t + 1 + n_past))
            o_ref[t, :, cs] = total / count - u_ref[t][:, cs]


def _pool_sample(state_t, u_t, n_past):
    n_new, b, c = u_t.shape
    return pl.pallas_call(
        functools.partial(_pool_sample_kernel, n_new=n_new, n_past=n_past),
        out_shape=jax.ShapeDtypeStruct((n_new, b, c), F32),
        compiler_params=pltpu.CompilerParams(vmem_limit_bytes=VMEM_LIMIT),
        name="pool_sample",
    )(state_t, u_t)


def _merge_kernel(oa_ref, ob_ref, df_ref, za_ref, zb_ref, zc_ref, ma_ref, mb_ref, mc_ref, x_ref,
                  woa_ref, wob_ref, woc_ref, wp_ref, ps_ref, wo_ref, gain_ref, bias_ref, y_ref, *, alpha):
    def silu(z):
        return z * _sigmoid(z)

    diff = df_ref[...]
    mixed = jnp.concatenate(
        [_dot(diff[:, g * POOL_GROUP_DIM:(g + 1) * POOL_GROUP_DIM].astype(BF16), wp_ref[g])
         for g in range(len(POOL_WINDOWS))], axis=1)
    o_c = mixed * ps_ref[...]
    y_a = _dot((oa_ref[...] * silu(za_ref[...])).astype(BF16), woa_ref[...])
    y_b = _dot((ob_ref[...] * silu(zb_ref[...])).astype(BF16), wob_ref[...])
    y_c = _dot((o_c * silu(zc_ref[...])).astype(BF16), woc_ref[...])
    merged = _sigmoid(ma_ref[...]) * y_a + _sigmoid(mb_ref[...]) * y_b + _sigmoid(mc_ref[...]) * y_c
    r = alpha * x_ref[...] + _dot(merged.astype(BF16), wo_ref[...])
    mu = jnp.mean(r, axis=1, keepdims=True)
    d = r - mu
    var = jnp.mean(d * d, axis=1, keepdims=True)
    y_ref[...] = d * lax.rsqrt(var + LN_EPS) * gain_ref[...] + bias_ref[...]


def _merge(o_a, o_b, diff, proj, x, wts, alpha, tm):
    rows, d_model = x.shape
    assert rows % tm == 0
    c = BRANCH_WIDTH
    gates_col0 = M_COL * c // d_model
    assert gates_col0 * d_model == M_COL * c

    def rowblk(width, colblk):
        return pl.BlockSpec((tm, width), lambda i: (i, colblk))

    def whole(a):
        return pl.BlockSpec(a.shape, lambda i: (0,) * a.ndim)

    return pl.pallas_call(
        functools.partial(_merge_kernel, alpha=alpha),
        grid=(rows // tm,),
        in_specs=[rowblk(c, 0), rowblk(c, 0), rowblk(c, 0),
                  rowblk(c, Z_COL), rowblk(c, Z_COL + 1), rowblk(c, Z_COL + 2),
                  rowblk(d_model, gates_col0), rowblk(d_model, gates_col0 + 1), rowblk(d_model, gates_col0 + 2),
                  rowblk(d_model, 0)] + [whole(a) for a in wts],
        out_specs=rowblk(d_model, 0),
        out_shape=jax.ShapeDtypeStruct((rows, d_model), F32),
        compiler_params=_params("arbitrary"),
        name="merge",
    )(o_a, o_b, diff, proj, proj, proj, proj, proj, proj, x, *wts)


def _decode_kernel(pt_ref, q_ref, kn_ref, vn_ref, k0_ref, k1_ref, v0_ref, v1_ref, oa_ref, ob_ref,
                   qb_ref, kn_pad_ref, vn_pad_ref, m_ref, l_ref, acc_ref, score_ref,
                   carry_ref, accb_ref, *, n_new, nblk, page):
    del pt_ref
    s = pl.program_id(1)
    width = 2 * BRANCH_WIDTH
    nrow = HEADS_PER_BRANCH * n_new
    half = BRANCH_WIDTH

    def attend(logits, values_a, values_b, mask_a, mask_b, slot):
        n = logits.shape[1]
        sa = logits[0:nrow]
        if mask_a is None:
            score_ref[slot] = jnp.sum(sa, axis=1, keepdims=True) / n
        else:
            sa = jnp.where(mask_a, sa, MASKED)
        m = jnp.max(sa, axis=1, keepdims=True)
        p = jnp.exp2(sa - m)
        m_ref[slot] = m
        l_ref[slot] = jnp.sum(p, axis=1, keepdims=True)
        acc_ref[slot] = values_a(p.astype(BF16))
        w, carry = _stick_break_tile(logits[nrow:2 * nrow], carry_ref[...], _strict_upper2(n), mask_b)
        carry_ref[...] = carry
        accb_ref[...] += values_b(w.astype(BF16))

    @pl.when(s == 0)
    def _start():
        q = q_ref[0]
        rows = [jnp.broadcast_to(q[t:t + 1, :], (HEADS_PER_BRANCH, width)) for t in range(n_new)]
        qrep = jnp.concatenate(rows + rows, axis=0)
        r = lax.broadcasted_iota(jnp.int32, (2 * nrow, width), 0)
        c = lax.broadcasted_iota(jnp.int32, (2 * nrow, width), 1)
        head = r % HEADS_PER_BRANCH + HEADS_PER_BRANCH * (r // nrow)
        qb_ref[...] = jnp.where(c // HEAD_DIM == head, qrep * (ATTN_SCALE * LOG2E), 0.0).astype(BF16)
        kn_pad_ref[...] = jnp.zeros_like(kn_pad_ref)
        vn_pad_ref[...] = jnp.zeros_like(vn_pad_ref)
        kn_pad_ref[0:n_new, :] = kn_ref[0]
        vn_pad_ref[0:n_new, :] = vn_ref[0]
        carry_ref[...] = jnp.zeros_like(carry_ref)
        accb_ref[...] = jnp.zeros_like(accb_ref)
        token = lax.broadcasted_iota(jnp.int32, (nrow, page), 0) // HEADS_PER_BRANCH
        key = lax.broadcasted_iota(jnp.int32, (nrow, page), 1)
        vn = vn_pad_ref[...].astype(BF16)
        attend(_dot_nt(qb_ref[...], kn_pad_ref[...].astype(BF16)),
               lambda p: _dot(p, vn[:, 0:half]), lambda w: _dot(w, vn[:, half:width]),
               key <= token, key < token, nblk)

    blk_id = nblk - 1 - s
    qb = qb_ref[...]
    logits = jnp.concatenate([_dot(qb, k0_ref[0, 0].astype(BF16)), _dot(qb, k1_ref[0, 0].astype(BF16))], axis=1)
    vt0 = v0_ref[0, 0].astype(BF16)
    vt1 = v1_ref[0, 0].astype(BF16)

    def values(lo, hi):
        return lambda p: _dot_nt(p[:, 0:page], vt0[lo:hi]) + _dot_nt(p[:, page:2 * page], vt1[lo:hi])

    attend(logits, values(0, half), values(half, width), None, None, blk_id)

    @pl.when(s == nblk - 1)
    def _finish():
        sc = [score_ref[j] for j in range(nblk)]
        sel = [jnp.zeros((nrow, 1), jnp.bool_) for _ in range(nblk)]
        for _ in range(min(MOBA_TOPK, nblk)):
            mx = functools.reduce(jnp.maximum, sc)
            found = jnp.zeros((nrow, 1), jnp.bool_)
            for j in range(nblk):
                pick = (sc[j] == mx) & (sc[j] > -jnp.inf) & jnp.logical_not(found)
                sel[j] = sel[j] | pick
                found = found | pick
                sc[j] = jnp.where(pick, -jnp.inf, sc[j])
        m_all = m_ref[nblk]
        for j in range(nblk):
            m_all = jnp.maximum(m_all, jnp.where(sel[j], m_ref[j], MASKED))
        wgt = jnp.exp2(m_ref[nblk] - m_all)
        l = wgt * l_ref[nblk]
        acc = wgt * acc_ref[nblk]
        for j in range(nblk):
            wgt = jnp.where(sel[j], jnp.exp2(m_ref[j] - m_all), 0.0)
            l = l + wgt * l_ref[j]
            acc = acc + wgt * acc_ref[j]
        r = lax.broadcasted_iota(jnp.int32, (nrow, half), 0)
        c = lax.broadcasted_iota(jnp.int32, (nrow, half), 1)
        own_head = c // HEAD_DIM == r % HEADS_PER_BRANCH

        def per_token(a):
            a = jnp.where(own_head, a, 0.0).reshape(n_new, HEADS_PER_BRANCH, half)
            return jnp.sum(a, axis=1)

        oa_ref[0] = per_token(acc / l)
        ob_ref[0] = per_token(accb_ref[...])


def _decode(proj, cache_kt, cache_vt, page_table, layer):
    b, n_new, _ = proj.shape
    n_pages = page_table.shape[1]
    width, page = cache_kt.shape[2:]
    assert width == 2 * BRANCH_WIDTH and 2 * page == MOBA_BLOCK and n_pages % 2 == 0
    assert n_new <= 8 and n_new <= page
    nblk = n_pages // 2
    nrow = HEADS_PER_BRANCH * n_new

    def new_spec(colblk):
        return pl.BlockSpec((1, n_new, width), lambda bi, s, pt: (bi, 0, colblk))

    def page_spec(which):
        return pl.BlockSpec((1, 1, width, page),
                            lambda bi, s, pt: (layer, pt[bi, 2 * (nblk - 1 - s) + which], 0, 0))

    out_spec = pl.BlockSpec((1, n_new, BRANCH_WIDTH), lambda bi, s, pt: (bi, 0, 0))
    grid_spec = pltpu.PrefetchScalarGridSpec(
        num_scalar_prefetch=1,
        grid=(b, nblk),
        in_specs=[new_spec(Q_COL // 2), new_spec(K_COL // 2), new_spec(V_COL // 2),
                  page_spec(0), page_spec(1), page_spec(0), page_spec(1)],
        out_specs=[out_spec, out_spec],
        scratch_shapes=[pltpu.VMEM((2 * nrow, width), BF16),
                        pltpu.VMEM((page, width), F32), pltpu.VMEM((page, width), F32),
                        pltpu.VMEM((nblk + 1, nrow, 1), F32), pltpu.VMEM((nblk + 1, nrow, 1), F32),
                        pltpu.VMEM((nblk + 1, nrow, BRANCH_WIDTH), F32), pltpu.VMEM((nblk, nrow, 1), F32),
                        pltpu.VMEM((nrow, 1), F32), pltpu.VMEM((nrow, BRANCH_WIDTH), F32)],
    )
    out = jax.ShapeDtypeStruct((b, n_new, BRANCH_WIDTH), F32)
    return pl.pallas_call(
        functools.partial(_decode_kernel, n_new=n_new, nblk=nblk, page=page),
        grid_spec=grid_spec,
        out_shape=[out, out],
        compiler_params=_params("arbitrary", "arbitrary"),
        name="decode",
    )(page_table, proj, proj, proj, cache_kt, cache_kt, cache_vt, cache_vt)


def _row_tile(rows, want):
    return want if rows % want == 0 else rows


def _layer(xp, xs, cache_kt, cache_vt, layer, state, page_table, w_in, merge_wts, alpha, sb_block):
    b, t, d = xp.shape
    bs, ts, _ = xs.shape
    n_in = w_in.shape[1]
    c = BRANCH_WIDTH

    proj_p = _inproj(xp.reshape(b * t, d), w_in, _row_tile(b * t, 512), 1024).reshape(b, t, n_in)
    proj_s = _inproj(xs.reshape(bs * ts, d), w_in, _row_tile(bs * ts, 512), 1024).reshape(bs, ts, n_in)

    oa_p = _moba_prompt(proj_p, ATTN_GROUP)
    ob_p = _sb_prompt(proj_p, sb_block, ATTN_GROUP)
    diff_p = _pool_prompt(proj_p, _row_tile(t, 256))
    yp = _merge(oa_p.reshape(b * t, c), ob_p.reshape(b * t, c), diff_p.reshape(b * t, c),
                proj_p.reshape(b * t, n_in), xp.reshape(b * t, d), merge_wts, alpha,
                _row_tile(b * t, 256)).reshape(b, t, d)

    past_len = page_table.shape[1] * cache_kt.shape[3]
    oa_s, ob_s = _decode(proj_s, cache_kt, cache_vt, page_table, layer)
    u_s = proj_s[:, :, U_COL * c:(U_COL + 1) * c]
    diff_s = _pool_sample(state.transpose(1, 0, 2), u_s.transpose(1, 0, 2), min(POOL_STATE, past_len))
    diff_s = diff_s.transpose(1, 0, 2)
    ys = _merge(oa_s.reshape(bs * ts, c), ob_s.reshape(bs * ts, c), diff_s.reshape(bs * ts, c),
                proj_s.reshape(bs * ts, n_in), xs.reshape(bs * ts, d), merge_wts, alpha,
                _row_tile(bs * ts, 256)).reshape(bs, ts, d)

    kv = 2 * c
    outs = dict(
        kp=proj_p[:, :, K_COL * c:K_COL * c + kv], vp=proj_p[:, :, V_COL * c:V_COL * c + kv],
        sp=proj_p[:, t - POOL_STATE:, U_COL * c:(U_COL + 1) * c],
        ks=proj_s[:, :, K_COL * c:K_COL * c + kv], vs=proj_s[:, :, V_COL * c:V_COL * c + kv],
        ss=jnp.concatenate([state, u_s], axis=1)[:, -POOL_STATE:])
    return yp, ys, outs


def kernel(x_prompt, x_sample, cache_k, cache_v, state_pool, page_table, w_in, w_out_a, w_out_b, w_out_c,
           w_pool, pool_scale, w_o, ln_gain, ln_bias):
    depth = w_in.shape[0]
    b, t, d = x_prompt.shape
    bs, ts, _ = x_sample.shape
    n_pool, page, n_heads, head_dim = cache_k.shape[1:]
    assert head_dim == HEAD_DIM and n_heads == 2 * HEADS_PER_BRANCH and t >= POOL_STATE
    alpha = (2 * depth) ** 0.25
    xp, xs = x_prompt, x_sample
    cache_kt = cache_k.transpose(0, 1, 3, 4, 2).reshape(depth, n_pool, n_heads * head_dim, page)
    cache_vt = cache_v.transpose(0, 1, 3, 4, 2).reshape(depth, n_pool, n_heads * head_dim, page)
    per_layer = []
    for l in range(depth):
        merge_wts = (w_out_a[l].astype(BF16), w_out_b[l].astype(BF16), w_out_c[l].astype(BF16),
                     w_pool[l].astype(BF16), pool_scale[l].reshape(1, -1), w_o[l].astype(BF16),
                     ln_gain[l].reshape(1, -1), ln_bias[l].reshape(1, -1))
        xp, xs, outs = _layer(xp, xs, cache_kt, cache_vt, l, state_pool[l],
                              page_table, w_in[l].astype(BF16), merge_wts, alpha, 256)
        per_layer.append(outs)

    def stack(name):
        return jnp.stack([o[name] for o in per_layer])

    page_shape = (depth, b, t // page, page, n_heads, head_dim)
    return (xp, xs,
            stack("kp").reshape(page_shape), stack("vp").reshape(page_shape), stack("sp"),
            stack("ks").reshape(depth, bs, ts, n_heads, head_dim),
            stack("vs").reshape(depth, bs, ts, n_heads, head_dim), stack("ss"))
```

```python
import functools

import jax
import jax.numpy as jnp
from jax import lax
from jax.experimental import pallas as pl
from jax.experimental.pallas import tpu as pltpu

F32 = jnp.float32
BF16 = jnp.bfloat16

HEAD_DIM = 64
HEADS_PER_BRANCH = 8
BRANCH_WIDTH = HEADS_PER_BRANCH * HEAD_DIM
LANES = 128
MOBA_BLOCK = 256
MOBA_TOPK = 3
POOL_WINDOWS = (2, 4, 8, 16)
POOL_STATE = max(POOL_WINDOWS) - 1
POOL_HALO = 16
POOL_GROUP_DIM = BRANCH_WIDTH // len(POOL_WINDOWS)
LN_EPS = 1e-5
MASKED = -1e30
UNDERFLOW_LOG2 = -160.0
ATTN_SCALE = HEAD_DIM ** -0.5
LOG2E = 1.4426950408889634
DECODE_STEP_BLOCKS = 2
MOBA_GROUP = 4
SB_GROUP = 1
VMEM_LIMIT = 48 * 1024 * 1024

Q_COL, K_COL, V_COL, U_COL, Z_COL, M_COL = 0, 2, 4, 6, 7, 10


def _dot(a, b):
    return jnp.dot(a, b, preferred_element_type=F32)


def _dot_nt(a, b):
    return lax.dot_general(a, b, (((1,), (1,)), ((), ())), preferred_element_type=F32)


def _sigmoid(x):
    return 1.0 / (1.0 + jnp.exp(-x))


def _params(*sem):
    return pltpu.CompilerParams(dimension_semantics=sem, vmem_limit_bytes=VMEM_LIMIT)


def _inproj_kernel(x_ref, w_ref, o_ref):
    o_ref[...] = _dot(x_ref[...].astype(BF16), w_ref[...])


def _inproj(x, w, tm, tn):
    m, k = x.shape
    n = w.shape[1]
    return pl.pallas_call(
        _inproj_kernel,
        grid=(n // tn, m // tm),
        in_specs=[pl.BlockSpec((tm, k), lambda j, i: (i, 0)),
                  pl.BlockSpec((k, tn), lambda j, i: (0, j))],
        out_specs=pl.BlockSpec((tm, tn), lambda j, i: (i, j)),
        out_shape=jax.ShapeDtypeStruct((m, n), F32),
        compiler_params=_params("arbitrary", "arbitrary"),
        name="inproj",
    )(x, w)


def _stick_break_tile(z, carry, neg_upper2, mask):
    n = jnp.maximum(z, 0.0) + jnp.log2(1.0 + jnp.exp2(-jnp.abs(z)))
    if mask is not None:
        n = jnp.where(mask, n, 0.0)
    hi = n.astype(BF16)
    lo = (n - hi.astype(F32)).astype(BF16)
    stay = _dot(jnp.concatenate([hi, lo], axis=1), neg_upper2)
    w = jnp.exp2(z + stay + carry)
    if mask is not None:
        w = jnp.where(mask, w, 0.0)
    return w, carry + stay[:, 0:1]


def _neg_upper2(n):
    j = lax.broadcasted_iota(jnp.int32, (2 * n, n), 0) % n
    s = lax.broadcasted_iota(jnp.int32, (2 * n, n), 1)
    return jnp.where(j >= s, -1.0, 0.0).astype(BF16)


def _split_heads(x, lane):
    return jnp.concatenate([jnp.where(lane < HEAD_DIM, x, 0.0), jnp.where(lane >= HEAD_DIM, x, 0.0)], axis=0)


def _stage_values(v_ref, vaug_ref, ngrp, gw):
    lane = lax.broadcasted_iota(jnp.int32, (gw, LANES), 1)

    def body(g, c):
        r = pl.multiple_of(g * gw, gw)
        vaug_ref[g] = _split_heads(v_ref[0, pl.ds(r, gw), :], lane).astype(BF16)
        return c

    lax.fori_loop(0, ngrp, body, 0)


def _moba_kernel(q_ref, k_ref, v_ref, o_ref, kaug_ref, vaug_ref, kmean_ref, *, nblk, grp):
    blk = MOBA_BLOCK
    gw = grp * blk
    i = pl.program_id(2)

    @pl.when(i == 0)
    def _prepare():
        kmean_ref[...] = jnp.zeros_like(kmean_ref)
        _stage_values(v_ref, vaug_ref, nblk // grp, gw)

        def body(j, c):
            r = pl.multiple_of(j * blk, blk)
            kb = k_ref[0, pl.ds(r, blk), :]
            kmean_ref[pl.ds(j, 1), :] = jnp.sum(kb, axis=0, keepdims=True) / blk
            onehot = (lax.broadcasted_iota(jnp.int32, (blk, LANES), 1) == j).astype(F32)
            kaug_ref[pl.ds(r, blk), :] = jnp.concatenate([kb, onehot], axis=1).astype(BF16)
            return c

        lax.fori_loop(0, nblk, body, 0)

    q2 = q_ref[0]
    lane = lax.broadcasted_iota(jnp.int32, (blk, HEAD_DIM), 1)
    lane_f = lane.astype(F32)
    lane2 = lax.broadcasted_iota(jnp.int32, (blk, LANES), 1)
    biases = []
    for h in range(2):
        qh = q2[:, h * HEAD_DIM:(h + 1) * HEAD_DIM]
        km = kmean_ref[:, h * HEAD_DIM:(h + 1) * HEAD_DIM]
        score = lax.dot_general(qh, km, (((1,), (1,)), ((), ())), precision=lax.Precision.HIGHEST,
                                preferred_element_type=F32)
        sc = jnp.where(lane < i, score, -jnp.inf)
        selected = lane == i
        for _ in range(MOBA_TOPK):
            mx = jnp.max(sc, axis=1, keepdims=True)
            is_max = (sc == mx) & (sc > -jnp.inf)
            first = jnp.min(jnp.where(is_max, lane_f, float(HEAD_DIM)), axis=1, keepdims=True)
            pick = lane_f == first
            selected = selected | pick
            sc = jnp.where(pick, -jnp.inf, sc)
        biases.append(jnp.where(selected, 0.0, MASKED))
    q_log2 = _split_heads(q2 * (ATTN_SCALE * LOG2E), lane2)
    pad = jnp.zeros((2 * blk, HEAD_DIM), F32)
    qaug = jnp.concatenate([q_log2, jnp.concatenate(biases, axis=0), pad], axis=1).astype(BF16)

    def scores(g):
        r = pl.multiple_of(g * gw, gw)
        return _dot_nt(qaug, kaug_ref[pl.ds(r, gw), :])

    def weighted_values(p, g):
        both = jnp.concatenate([p[0:blk], p[blk:2 * blk]], axis=1).astype(BF16)
        return _dot(both, vaug_ref[g])

    def per_head(x):
        return jnp.where(lane2 < HEAD_DIM, x[0:blk], x[blk:2 * blk])

    g_own = i // grp
    q_pos = i * blk + lax.broadcasted_iota(jnp.int32, (2 * blk, gw), 0) % blk
    k_pos = g_own * gw + lax.broadcasted_iota(jnp.int32, (2 * blk, gw), 1)
    s = jnp.where(k_pos <= q_pos, scores(g_own), MASKED)
    m = jnp.max(s, axis=1, keepdims=True)
    p = jnp.exp2(s - m)
    l = jnp.sum(p, axis=1, keepdims=True)
    acc = weighted_values(p, g_own)

    def body(g, carry):
        m, l, acc = carry
        s = scores(g)
        m_new = jnp.maximum(m, jnp.max(s, axis=1, keepdims=True))
        alpha = jnp.exp2(m - m_new)
        p = jnp.exp2(s - m_new)
        l = alpha * l + jnp.sum(p, axis=1, keepdims=True)
        acc = per_head(alpha) * acc + weighted_values(p, g)
        return m_new, l, acc

    m, l, acc = lax.fori_loop(0, g_own, body, (m, l, acc))
    o_ref[0] = acc / per_head(l)


def _moba_prompt(proj, grp):
    b, t, _ = proj.shape
    blk = MOBA_BLOCK
    nblk = t // blk
    assert t % (grp * blk) == 0 and nblk <= HEAD_DIM
    npair = BRANCH_WIDTH // LANES
    col0 = BRANCH_WIDTH // LANES
    return pl.pallas_call(
        functools.partial(_moba_kernel, nblk=nblk, grp=grp),
        grid=(b, npair, nblk),
        in_specs=[pl.BlockSpec((1, blk, LANES), lambda bi, hp, i: (bi, i, Q_COL * col0 + hp)),
                  pl.BlockSpec((1, t, LANES), lambda bi, hp, i: (bi, 0, K_COL * col0 + hp)),
                  pl.BlockSpec((1, t, LANES), lambda bi, hp, i: (bi, 0, V_COL * col0 + hp))],
        out_specs=pl.BlockSpec((1, blk, LANES), lambda bi, hp, i: (bi, i, hp)),
        out_shape=jax.ShapeDtypeStruct((b, t, BRANCH_WIDTH), F32),
        scratch_shapes=[pltpu.VMEM((t, 2 * LANES), BF16), pltpu.VMEM((nblk // grp, 2 * grp * blk, LANES), BF16),
                        pltpu.VMEM((HEAD_DIM, LANES), F32)],
        compiler_params=_params("arbitrary", "arbitrary", "arbitrary"),
        name="moba_prompt",
    )(proj, proj, proj)


def _sb_kernel(q_ref, k_ref, v_ref, o_ref, kb_ref, vaug_ref, *, nblk, blk, grp):
    gw = grp * blk
    i = pl.program_id(2)

    @pl.when(i == 0)
    def _prepare():
        _stage_values(v_ref, vaug_ref, nblk // grp, gw)

        def body(j, c):
            r = pl.multiple_of(j * blk, blk)
            kb_ref[pl.ds(r, blk), :] = k_ref[0, pl.ds(r, blk), :].astype(BF16)
            return c

        lax.fori_loop(0, nblk, body, 0)

    lane = lax.broadcasted_iota(jnp.int32, (blk, LANES), 1)
    qs = _split_heads(q_ref[0] * (ATTN_SCALE * LOG2E), lane).astype(BF16)
    upper2 = _neg_upper2(blk)

    def tile(g, carry, q_pos):
        r = pl.multiple_of(g * gw, gw)
        z = _dot_nt(qs, kb_ref[pl.ds(r, gw), :])
        ws = [None] * grp
        for b in reversed(range(grp)):
            mask = None
            if q_pos is not None:
                k_pos = g * gw + b * blk + lax.broadcasted_iota(jnp.int32, (2 * blk, blk), 1)
                mask = k_pos < q_pos
            ws[b], carry = _stick_break_tile(z[:, b * blk:(b + 1) * blk], carry, upper2, mask)
        w = jnp.concatenate(ws, axis=1)
        both = jnp.concatenate([w[0:blk], w[blk:2 * blk]], axis=1).astype(BF16)
        return carry, _dot(both, vaug_ref[g])

    g_own = i // grp
    q_pos = i * blk + lax.broadcasted_iota(jnp.int32, (2 * blk, blk), 0) % blk
    carry, acc = tile(g_own, jnp.zeros((2 * blk, 1), F32), q_pos)

    def more(state):
        step, carry, _ = state
        return (step < g_own) & (jnp.max(carry) > UNDERFLOW_LOG2)

    def body(state):
        step, carry, acc = state
        carry, part = tile(g_own - 1 - step, carry, None)
        return step + 1, carry, acc + part

    _, _, acc = lax.while_loop(more, body, (0, carry, acc))
    o_ref[0] = acc


def _sb_prompt(proj, blk, grp):
    b, t, _ = proj.shape
    nblk = t // blk
    assert t % (grp * blk) == 0
    npair = BRANCH_WIDTH // LANES
    col0 = BRANCH_WIDTH // LANES
    return pl.pallas_call(
        functools.partial(_sb_kernel, nblk=nblk, blk=blk, grp=grp),
        grid=(b, npair, nblk),
        in_specs=[pl.BlockSpec((1, blk, LANES), lambda bi, hp, i: (bi, i, (Q_COL + 1) * col0 + hp)),
                  pl.BlockSpec((1, t, LANES), lambda bi, hp, i: (bi, 0, (K_COL + 1) * col0 + hp)),
                  pl.BlockSpec((1, t, LANES), lambda bi, hp, i: (bi, 0, (V_COL + 1) * col0 + hp))],
        out_specs=pl.BlockSpec((1, blk, LANES), lambda bi, hp, i: (bi, i, hp)),
        out_shape=jax.ShapeDtypeStruct((b, t, BRANCH_WIDTH), F32),
        scratch_shapes=[pltpu.VMEM((t, LANES), BF16), pltpu.VMEM((nblk // grp, 2 * grp * blk, LANES), BF16)],
        compiler_params=_params("arbitrary", "arbitrary", "arbitrary"),
        name="sb_prompt",
    )(proj, proj, proj)


def _pool_prompt_kernel(u_ref, halo_ref, o_ref, ext_ref, *, tm):
    i = pl.program_id(1)
    ext_ref[0:POOL_HALO, :] = jnp.where(i == 0, 0.0, halo_ref[0])
    ext_ref[POOL_HALO:POOL_HALO + tm, :] = u_ref[0]
    t = i * tm + lax.broadcasted_iota(jnp.int32, (tm, 1), 0)
    for g, w in enumerate(POOL_WINDOWS):
        cs = slice(g * POOL_GROUP_DIM, (g + 1) * POOL_GROUP_DIM)
        u = ext_ref[POOL_HALO:POOL_HALO + tm, cs]
        total = u
        for k in range(1, w):
            total = total + ext_ref[POOL_HALO - k:POOL_HALO - k + tm, cs]
        count = jnp.minimum(w, t + 1).astype(F32)
        o_ref[0, :, cs] = total / count - u


def _pool_prompt(proj, tm):
    b, t, _ = proj.shape
    assert t % tm == 0 and tm % POOL_HALO == 0
    halo_blocks = tm // POOL_HALO
    return pl.pallas_call(
        functools.partial(_pool_prompt_kernel, tm=tm),
        grid=(b, t // tm),
        in_specs=[pl.BlockSpec((1, tm, BRANCH_WIDTH), lambda bi, i: (bi, i, U_COL)),
                  pl.BlockSpec((1, POOL_HALO, BRANCH_WIDTH),
                               lambda bi, i: (bi, jnp.maximum(i * halo_blocks - 1, 0), U_COL))],
        out_specs=pl.BlockSpec((1, tm, BRANCH_WIDTH), lambda bi, i: (bi, i, 0)),
        out_shape=jax.ShapeDtypeStruct((b, t, BRANCH_WIDTH), F32),
        scratch_shapes=[pltpu.VMEM((POOL_HALO + tm, BRANCH_WIDTH), F32)],
        compiler_params=_params("arbitrary", "arbitrary"),
        name="pool_prompt",
    )(proj, proj)


def _pool_sample_kernel(state_ref, u_ref, o_ref, *, n_new, n_past):
    def ext(r):
        return state_ref[r] if r < POOL_STATE else u_ref[r - POOL_STATE]

    for t in range(n_new):
        for g, w in enumerate(POOL_WINDOWS):
            cs = slice(g * POOL_GROUP_DIM, (g + 1) * POOL_GROUP_DIM)
            total = u_ref[t][:, cs]
            for k in range(1, w):
                total = total + ext(POOL_STATE + t - k)[:, cs]
            count = float(min(w, t + 1 + n_past))
            o_ref[t, :, cs] = total / count - u_ref[t][:, cs]


def _pool_sample(state_t, u_t, n_past):
    n_new, b, c = u_t.shape
    return pl.pallas_call(
        functools.partial(_pool_sample_kernel, n_new=n_new, n_past=n_past),
        out_shape=jax.ShapeDtypeStruct((n_new, b, c), F32),
        compiler_params=pltpu.CompilerParams(vmem_limit_bytes=VMEM_LIMIT),
        name="pool_sample",
    )(state_t, u_t)


def _merge_kernel(oa_ref, ob_ref, df_ref, za_ref, zb_ref, zc_ref, ma_ref, mb_ref, mc_ref, x_ref,
                  woa_ref, wob_ref, woc_ref, wp_ref, ps_ref, wo_ref, gain_ref, bias_ref, y_ref, *, alpha):
    def silu(z):
        return z * _sigmoid(z)

    diff = df_ref[...]
    mixed = jnp.concatenate(
        [_dot(diff[:, g * POOL_GROUP_DIM:(g + 1) * POOL_GROUP_DIM].astype(BF16), wp_ref[g])
         for g in range(len(POOL_WINDOWS))], axis=1)
    o_c = mixed * ps_ref[...]
    y_a = _dot((oa_ref[...] * silu(za_ref[...])).astype(BF16), woa_ref[...])
    y_b = _dot((ob_ref[...] * silu(zb_ref[...])).astype(BF16), wob_ref[...])
    y_c = _dot((o_c * silu(zc_ref[...])).astype(BF16), woc_ref[...])
    merged = _sigmoid(ma_ref[...]) * y_a + _sigmoid(mb_ref[...]) * y_b + _sigmoid(mc_ref[...]) * y_c
    r = alpha * x_ref[...] + _dot(merged.astype(BF16), wo_ref[...])
    mu = jnp.mean(r, axis=1, keepdims=True)
    d = r - mu
    var = jnp.mean(d * d, axis=1, keepdims=True)
    y_ref[...] = d * lax.rsqrt(var + LN_EPS) * gain_ref[...] + bias_ref[...]


def _merge(o_a, o_b, diff, proj, x, wts, alpha, tm):
    rows, d_model = x.shape
    assert rows % tm == 0
    c = BRANCH_WIDTH
    gates_col0 = M_COL * c // d_model
    assert gates_col0 * d_model == M_COL * c

    def rowblk(width, colblk):
        return pl.BlockSpec((tm, width), lambda i: (i, colblk))

    def whole(a):
        return pl.BlockSpec(a.shape, lambda i: (0,) * a.ndim)

    return pl.pallas_call(
        functools.partial(_merge_kernel, alpha=alpha),
        grid=(rows // tm,),
        in_specs=[rowblk(c, 0), rowblk(c, 0), rowblk(c, 0),
                  rowblk(c, Z_COL), rowblk(c, Z_COL + 1), rowblk(c, Z_COL + 2),
                  rowblk(d_model, gates_col0), rowblk(d_model, gates_col0 + 1), rowblk(d_model, gates_col0 + 2),
                  rowblk(d_model, 0)] + [whole(a) for a in wts],
        out_specs=rowblk(d_model, 0),
        out_shape=jax.ShapeDtypeStruct((rows, d_model), F32),
        compiler_params=_params("arbitrary"),
        name="merge",
    )(o_a, o_b, diff, proj, proj, proj, proj, proj, proj, x, *wts)


def _decode_kernel(pt_ref, q_ref, kn_ref, vn_ref, *refs, n_new, nblk, page, step_blocks):
    del pt_ref
    n_pages = 2 * step_blocks
    k_refs, v_refs = refs[0:n_pages], refs[n_pages:2 * n_pages]
    (oa_ref, ob_ref, qb_ref, kn_pad_ref, vn_pad_ref, m_ref, l_ref, acc_ref, score_ref,
     carry_ref, accb_ref) = refs[2 * n_pages:]
    s = pl.program_id(1)
    n_steps = nblk // step_blocks
    width = 2 * BRANCH_WIDTH
    nrow = HEADS_PER_BRANCH * n_new
    half = BRANCH_WIDTH

    def attend(logits_a, logits_b, values_a, values_b, mask_a, mask_b, slot):
        sa = logits_a()
        n = sa.shape[1]
        if mask_a is None:
            score_ref[slot] = jnp.sum(sa, axis=1, keepdims=True) / n
        else:
            sa = jnp.where(mask_a, sa, MASKED)
        m = jnp.max(sa, axis=1, keepdims=True)
        p = jnp.exp2(sa - m)
        m_ref[slot] = m
        l_ref[slot] = jnp.sum(p, axis=1, keepdims=True)
        acc_ref[slot] = values_a(p.astype(BF16))

        @pl.when(jnp.max(carry_ref[...]) > UNDERFLOW_LOG2)
        def _stick_breaking():
            w, carry = _stick_break_tile(logits_b(), carry_ref[...], _neg_upper2(n), mask_b)
            carry_ref[...] = carry
            accb_ref[...] += values_b(w.astype(BF16))

    @pl.when(s == 0)
    def _start():
        q = q_ref[0]
        rows = [jnp.broadcast_to(q[t:t + 1, :], (HEADS_PER_BRANCH, width)) for t in range(n_new)]
        qrep = jnp.concatenate(rows + rows, axis=0)
        r = lax.broadcasted_iota(jnp.int32, (2 * nrow, width), 0)
        c = lax.broadcasted_iota(jnp.int32, (2 * nrow, width), 1)
        head = r % HEADS_PER_BRANCH + HEADS_PER_BRANCH * (r // nrow)
        qb_ref[...] = jnp.where(c // HEAD_DIM == head, qrep * (ATTN_SCALE * LOG2E), 0.0).astype(BF16)
        kn_pad_ref[...] = jnp.zeros_like(kn_pad_ref)
        vn_pad_ref[...] = jnp.zeros_like(vn_pad_ref)
        kn_pad_ref[0:n_new, :] = kn_ref[0]
        vn_pad_ref[0:n_new, :] = vn_ref[0]
        carry_ref[...] = jnp.zeros_like(carry_ref)
        accb_ref[...] = jnp.zeros_like(accb_ref)
        token = lax.broadcasted_iota(jnp.int32, (nrow, page), 0) // HEADS_PER_BRANCH
        key = lax.broadcasted_iota(jnp.int32, (nrow, page), 1)

        def new_logits(branch):
            rows_, cols_ = slice(branch * nrow, (branch + 1) * nrow), slice(branch * half, (branch + 1) * half)
            return lambda: _dot_nt(qb_ref[rows_, cols_], kn_pad_ref[:, cols_].astype(BF16))

        def new_values(branch):
            cols_ = slice(branch * half, (branch + 1) * half)
            return lambda w: _dot(w, vn_pad_ref[:, cols_].astype(BF16))

        attend(new_logits(0), new_logits(1), new_values(0), new_values(1), key <= token, key < token, nblk)

    def page_logits(branch, k0_ref, k1_ref):
        rows_, feats = slice(branch * nrow, (branch + 1) * nrow), slice(branch * half, (branch + 1) * half)

        def f():
            qh = qb_ref[rows_, feats]
            return jnp.concatenate([_dot(qh, k0_ref[0, 0, feats, :].astype(BF16)),
                                    _dot(qh, k1_ref[0, 0, feats, :].astype(BF16))], axis=1)
        return f

    def page_values(branch, v0_ref, v1_ref):
        feats = slice(branch * half, (branch + 1) * half)
        return lambda w: (_dot_nt(w[:, 0:page], v0_ref[0, 0, feats, :].astype(BF16))
                          + _dot_nt(w[:, page:2 * page], v1_ref[0, 0, feats, :].astype(BF16)))

    for j in range(step_blocks):
        kk, vv = k_refs[2 * j:2 * j + 2], v_refs[2 * j:2 * j + 2]
        attend(page_logits(0, *kk), page_logits(1, *kk), page_values(0, *vv), page_values(1, *vv),
               None, None, nblk - 1 - (s * step_blocks + j))

    @pl.when(s == n_steps - 1)
    def _finish():
        sc = [score_ref[j] for j in range(nblk)]
        sel = [jnp.zeros((nrow, 1), jnp.bool_) for _ in range(nblk)]
        for _ in range(min(MOBA_TOPK, nblk)):
            mx = functools.reduce(jnp.maximum, sc)
            found = jnp.zeros((nrow, 1), jnp.bool_)
            for j in range(nblk):
                pick = (sc[j] == mx) & (sc[j] > -jnp.inf) & jnp.logical_not(found)
                sel[j] = sel[j] | pick
                found = found | pick
                sc[j] = jnp.where(pick, -jnp.inf, sc[j])
        m_all = m_ref[nblk]
        for j in range(nblk):
            m_all = jnp.maximum(m_all, jnp.where(sel[j], m_ref[j], MASKED))
        wgt = jnp.exp2(m_ref[nblk] - m_all)
        l = wgt * l_ref[nblk]
        acc = wgt * acc_ref[nblk]
        for j in range(nblk):
            wgt = jnp.where(sel[j], jnp.exp2(m_ref[j] - m_all), 0.0)
            l = l + wgt * l_ref[j]
            acc = acc + wgt * acc_ref[j]
        r = lax.broadcasted_iota(jnp.int32, (nrow, half), 0)
        c = lax.broadcasted_iota(jnp.int32, (nrow, half), 1)
        own_head = c // HEAD_DIM == r % HEADS_PER_BRANCH

        def per_token(a):
            a = jnp.where(own_head, a, 0.0).reshape(n_new, HEADS_PER_BRANCH, half)
            return jnp.sum(a, axis=1)

        oa_ref[0] = per_token(acc / l)
        ob_ref[0] = per_token(accb_ref[...])


def _decode(proj, cache_kt, cache_vt, page_table, layer):
    b, n_new, _ = proj.shape
    n_pages = page_table.shape[1]
    width, page = cache_kt.shape[2:]
    assert width == 2 * BRANCH_WIDTH and 2 * page == MOBA_BLOCK and n_pages % 2 == 0
    assert n_new <= 8 and n_new <= page
    nblk = n_pages // 2
    step_blocks = DECODE_STEP_BLOCKS if nblk % DECODE_STEP_BLOCKS == 0 else 1
    nrow = HEADS_PER_BRANCH * n_new

    def new_spec(colblk):
        return pl.BlockSpec((1, n_new, width), lambda bi, s, pt: (bi, 0, colblk))

    def page_spec(j, which):
        return pl.BlockSpec((1, 1, width, page),
                            lambda bi, s, pt: (layer, pt[bi, 2 * (nblk - 1 - (s * step_blocks + j)) + which], 0, 0))

    page_specs = [page_spec(j, which) for j in range(step_blocks) for which in range(2)]

    out_spec = pl.BlockSpec((1, n_new, BRANCH_WIDTH), lambda bi, s, pt: (bi, 0, 0))
    grid_spec = pltpu.PrefetchScalarGridSpec(
        num_scalar_prefetch=1,
        grid=(b, nblk // step_blocks),
        in_specs=[new_spec(Q_COL // 2), new_spec(K_COL // 2), new_spec(V_COL // 2)] + page_specs + page_specs,
        out_specs=[out_spec, out_spec],
        scratch_shapes=[pltpu.VMEM((2 * nrow, width), BF16),
                        pltpu.VMEM((page, width), F32), pltpu.VMEM((page, width), F32),
                        pltpu.VMEM((nblk + 1, nrow, 1), F32), pltpu.VMEM((nblk + 1, nrow, 1), F32),
                        pltpu.VMEM((nblk + 1, nrow, BRANCH_WIDTH), F32), pltpu.VMEM((nblk, nrow, 1), F32),
                        pltpu.VMEM((nrow, 1), F32), pltpu.VMEM((nrow, BRANCH_WIDTH), F32)],
    )
    out = jax.ShapeDtypeStruct((b, n_new, BRANCH_WIDTH), F32)
    return pl.pallas_call(
        functools.partial(_decode_kernel, n_new=n_new, nblk=nblk, page=page, step_blocks=step_blocks),
        grid_spec=grid_spec,
        out_shape=[out, out],
        compiler_params=_params("arbitrary", "arbitrary"),
        name="decode",
    )(page_table, proj, proj, proj, *([cache_kt] * (2 * step_blocks)), *([cache_vt] * (2 * step_blocks)))


def _row_tile(rows, want):
    return want if rows % want == 0 else rows


def _layer(xp, xs, cache_kt, cache_vt, layer, state, page_table, w_in, merge_wts, alpha, sb_block):
    b, t, d = xp.shape
    bs, ts, _ = xs.shape
    n_in = w_in.shape[1]
    c = BRANCH_WIDTH

    proj_p = _inproj(xp.reshape(b * t, d), w_in, _row_tile(b * t, 512), 1024).reshape(b, t, n_in)
    proj_s = _inproj(xs.reshape(bs * ts, d), w_in, _row_tile(bs * ts, 512), 1024).reshape(bs, ts, n_in)

    oa_p = _moba_prompt(proj_p, MOBA_GROUP)
    ob_p = _sb_prompt(proj_p, sb_block, SB_GROUP)
    diff_p = _pool_prompt(proj_p, _row_tile(t, 256))
    yp = _merge(oa_p.reshape(b * t, c), ob_p.reshape(b * t, c), diff_p.reshape(b * t, c),
                proj_p.reshape(b * t, n_in), xp.reshape(b * t, d), merge_wts, alpha,
                _row_tile(b * t, 256)).reshape(b, t, d)

    past_len = page_table.shape[1] * cache_kt.shape[3]
    oa_s, ob_s = _decode(proj_s, cache_kt, cache_vt, page_table, layer)
    state_t = state.transpose(1, 0, 2)
    u_t = proj_s[:, :, U_COL * c:(U_COL + 1) * c].transpose(1, 0, 2)
    diff_s = _pool_sample(state_t, u_t, min(POOL_STATE, past_len)).transpose(1, 0, 2)
    ys = _merge(oa_s.reshape(bs * ts, c), ob_s.reshape(bs * ts, c), diff_s.reshape(bs * ts, c),
                proj_s.reshape(bs * ts, n_in), xs.reshape(bs * ts, d), merge_wts, alpha,
                _row_tile(bs * ts, 256)).reshape(bs, ts, d)

    kv = 2 * c
    outs = dict(
        kp=proj_p[:, :, K_COL * c:K_COL * c + kv], vp=proj_p[:, :, V_COL * c:V_COL * c + kv],
        sp=proj_p[:, t - POOL_STATE:, U_COL * c:(U_COL + 1) * c],
        ks=proj_s[:, :, K_COL * c:K_COL * c + kv], vs=proj_s[:, :, V_COL * c:V_COL * c + kv],
        ss=jnp.concatenate([state_t, u_t], axis=0)[-POOL_STATE:].transpose(1, 0, 2))
    return yp, ys, outs


def kernel(x_prompt, x_sample, cache_k, cache_v, state_pool, page_table, w_in, w_out_a, w_out_b, w_out_c,
           w_pool, pool_scale, w_o, ln_gain, ln_bias):
    depth = w_in.shape[0]
    b, t, d = x_prompt.shape
    bs, ts, _ = x_sample.shape
    n_pool, page, n_heads, head_dim = cache_k.shape[1:]
    assert head_dim == HEAD_DIM and n_heads == 2 * HEADS_PER_BRANCH and t >= POOL_STATE
    alpha = (2 * depth) ** 0.25
    xp, xs = x_prompt, x_sample
    cache_kt = cache_k.transpose(0, 1, 3, 4, 2).reshape(depth, n_pool, n_heads * head_dim, page)
    cache_vt = cache_v.transpose(0, 1, 3, 4, 2).reshape(depth, n_pool, n_heads * head_dim, page)
    per_layer = []
    for l in range(depth):
        merge_wts = (w_out_a[l].astype(BF16), w_out_b[l].astype(BF16), w_out_c[l].astype(BF16),
                     w_pool[l].astype(BF16), pool_scale[l].reshape(1, -1), w_o[l].astype(BF16),
                     ln_gain[l].reshape(1, -1), ln_bias[l].reshape(1, -1))
        xp, xs, outs = _layer(xp, xs, cache_kt, cache_vt, l, state_pool[l],
                              page_table, w_in[l].astype(BF16), merge_wts, alpha, 256)
        per_layer.append(outs)

    def stack(name):
        return jnp.stack([o[name] for o in per_layer])

    page_shape = (depth, b, t // page, page, n_heads, head_dim)
    return (xp, xs,
            stack("kp").reshape(page_shape), stack("vp").reshape(page_shape), stack("sp"),
            stack("ks").reshape(depth, bs, ts, n_heads, head_dim),
            stack("vs").reshape(depth, bs, ts, n_heads, head_dim), stack("ss"))
```

```python
import functools

import jax
import jax.numpy as jnp
from jax import lax
from jax.experimental import pallas as pl
from jax.experimental.pallas import tpu as pltpu

F32 = jnp.float32
BF16 = jnp.bfloat16

HEAD_DIM = 64
HEADS_PER_BRANCH = 8
BRANCH_WIDTH = HEADS_PER_BRANCH * HEAD_DIM
LANES = 128
MOBA_BLOCK = 256
MOBA_TOPK = 3
POOL_WINDOWS = (2, 4, 8, 16)
POOL_STATE = max(POOL_WINDOWS) - 1
POOL_HALO = 16
POOL_GROUP_DIM = BRANCH_WIDTH // len(POOL_WINDOWS)
LN_EPS = 1e-5
MASKED = -1e30
UNDERFLOW_LOG2 = -160.0
ATTN_SCALE = HEAD_DIM ** -0.5
LOG2E = 1.4426950408889634
DECODE_STEP_BLOCKS = 4
MOBA_GROUP = 4
SB_GROUP = 1
VMEM_LIMIT = 48 * 1024 * 1024

Q_COL, K_COL, V_COL, U_COL, Z_COL, M_COL = 0, 2, 4, 6, 7, 10


def _dot(a, b):
    return jnp.dot(a, b, preferred_element_type=F32)


def _dot_nt(a, b):
    return lax.dot_general(a, b, (((1,), (1,)), ((), ())), preferred_element_type=F32)


def _sigmoid(x):
    return 1.0 / (1.0 + jnp.exp(-x))


def _params(*sem):
    return pltpu.CompilerParams(dimension_semantics=sem, vmem_limit_bytes=VMEM_LIMIT)


def _inproj_kernel(x_ref, w_ref, o_ref):
    o_ref[...] = _dot(x_ref[...].astype(BF16), w_ref[...])


def _inproj(x, w, tm, tn):
    m, k = x.shape
    n = w.shape[1]
    return pl.pallas_call(
        _inproj_kernel,
        grid=(n // tn, m // tm),
        in_specs=[pl.BlockSpec((tm, k), lambda j, i: (i, 0)),
                  pl.BlockSpec((k, tn), lambda j, i: (0, j))],
        out_specs=pl.BlockSpec((tm, tn), lambda j, i: (i, j)),
        out_shape=jax.ShapeDtypeStruct((m, n), F32),
        compiler_params=_params("arbitrary", "arbitrary"),
        name="inproj",
    )(x, w)


def _kv_pages_kernel(x0_ref, x1_ref, wk_ref, wv_ref, k_ref, v_ref, *, page):
    layer = pl.program_id(0)

    def emit(x_ref):
        x = x_ref[...].astype(BF16)
        for w_ref, o_ref in ((wk_ref, k_ref), (wv_ref, v_ref)):
            t = _dot_nt(w_ref[0], x)
            for p in range(x.shape[0] // page):
                o_ref[0, p] = t[:, p * page:(p + 1) * page]

    pl.when(layer == 0)(lambda: emit(x0_ref))
    pl.when(layer != 0)(lambda: emit(x1_ref))


def _kv_pages(xs, wk_t, wv_t, page, tm):
    depth, feat, d = wk_t.shape
    rows = xs[0].shape[0]
    assert depth == len(xs) == 2 and rows % tm == 0 and tm % page == 0
    n_tiles = rows // tm

    def x_spec(which):
        return pl.BlockSpec((tm, d), lambda l, i: (jnp.where(l == which, i, (n_tiles - 1) * (1 - which)), 0))

    w_spec = pl.BlockSpec((1, feat, d), lambda l, i: (l, 0, 0))
    out_spec = pl.BlockSpec((1, tm // page, feat, page), lambda l, i: (l, i, 0, 0))
    out = jax.ShapeDtypeStruct((depth, rows // page, feat, page), F32)
    return pl.pallas_call(
        functools.partial(_kv_pages_kernel, page=page),
        grid=(depth, n_tiles),
        in_specs=[x_spec(0), x_spec(1), w_spec, w_spec],
        out_specs=[out_spec, out_spec],
        out_shape=[out, out],
        compiler_params=_params("arbitrary", "arbitrary"),
        name="kv_pages",
    )(xs[0], xs[1], wk_t, wv_t)


def _stick_break_tile(z, carry, neg_upper2, mask):
    n = jnp.maximum(z, 0.0) + jnp.log2(1.0 + jnp.exp2(-jnp.abs(z)))
    if mask is not None:
        n = jnp.where(mask, n, 0.0)
    hi = n.astype(BF16)
    lo = (n - hi.astype(F32)).astype(BF16)
    stay = _dot(jnp.concatenate([hi, lo], axis=1), neg_upper2)
    w = jnp.exp2(z + stay + carry)
    if mask is not None:
        w = jnp.where(mask, w, 0.0)
    return w, carry + stay[:, 0:1]


def _neg_upper2(n):
    j = lax.broadcasted_iota(jnp.int32, (2 * n, n), 0) % n
    s = lax.broadcasted_iota(jnp.int32, (2 * n, n), 1)
    return jnp.where(j >= s, -1.0, 0.0).astype(BF16)


def _split_heads(x, lane):
    return jnp.concatenate([jnp.where(lane < HEAD_DIM, x, 0.0), jnp.where(lane >= HEAD_DIM, x, 0.0)], axis=0)


def _stage_values(v_ref, vaug_ref, ngrp, gw):
    lane = lax.broadcasted_iota(jnp.int32, (gw, LANES), 1)

    def body(g, c):
        r = pl.multiple_of(g * gw, gw)
        vaug_ref[g] = _split_heads(v_ref[0, pl.ds(r, gw), :], lane).astype(BF16)
        return c

    lax.fori_loop(0, ngrp, body, 0)


def _moba_kernel(q_ref, k_ref, v_ref, o_ref, kaug_ref, vaug_ref, kmean_ref, *, nblk, grp):
    blk = MOBA_BLOCK
    gw = grp * blk
    i = pl.program_id(2)

    @pl.when(i == 0)
    def _prepare():
        kmean_ref[...] = jnp.zeros_like(kmean_ref)
        _stage_values(v_ref, vaug_ref, nblk // grp, gw)

        def body(j, c):
            r = pl.multiple_of(j * blk, blk)
            kb = k_ref[0, pl.ds(r, blk), :]
            kmean_ref[pl.ds(j, 1), :] = jnp.sum(kb, axis=0, keepdims=True) / blk
            onehot = (lax.broadcasted_iota(jnp.int32, (blk, LANES), 1) == j).astype(F32)
            kaug_ref[pl.ds(r, blk), :] = jnp.concatenate([kb, onehot], axis=1).astype(BF16)
            return c

        lax.fori_loop(0, nblk, body, 0)

    q2 = q_ref[0]
    lane = lax.broadcasted_iota(jnp.int32, (blk, HEAD_DIM), 1)
    lane_f = lane.astype(F32)
    lane2 = lax.broadcasted_iota(jnp.int32, (blk, LANES), 1)
    biases = []
    for h in range(2):
        qh = q2[:, h * HEAD_DIM:(h + 1) * HEAD_DIM]
        km = kmean_ref[:, h * HEAD_DIM:(h + 1) * HEAD_DIM]
        score = lax.dot_general(qh, km, (((1,), (1,)), ((), ())), precision=lax.Precision.HIGHEST,
                                preferred_element_type=F32)
        sc = jnp.where(lane < i, score, -jnp.inf)
        selected = lane == i
        for _ in range(MOBA_TOPK):
            mx = jnp.max(sc, axis=1, keepdims=True)
            is_max = (sc == mx) & (sc > -jnp.inf)
            first = jnp.min(jnp.where(is_max, lane_f, float(HEAD_DIM)), axis=1, keepdims=True)
            pick = lane_f == first
            selected = selected | pick
            sc = jnp.where(pick, -jnp.inf, sc)
        biases.append(jnp.where(selected, 0.0, MASKED))
    q_log2 = _split_heads(q2 * (ATTN_SCALE * LOG2E), lane2)
    pad = jnp.zeros((2 * blk, HEAD_DIM), F32)
    qaug = jnp.concatenate([q_log2, jnp.concatenate(biases, axis=0), pad], axis=1).astype(BF16)

    def scores(g):
        r = pl.multiple_of(g * gw, gw)
        return _dot_nt(qaug, kaug_ref[pl.ds(r, gw), :])

    def weighted_values(p, g):
        both = jnp.concatenate([p[0:blk], p[blk:2 * blk]], axis=1).astype(BF16)
        return _dot(both, vaug_ref[g])

    def per_head(x):
        return jnp.where(lane2 < HEAD_DIM, x[0:blk], x[blk:2 * blk])

    g_own = i // grp
    q_pos = i * blk + lax.broadcasted_iota(jnp.int32, (2 * blk, gw), 0) % blk
    k_pos = g_own * gw + lax.broadcasted_iota(jnp.int32, (2 * blk, gw), 1)
    s = jnp.where(k_pos <= q_pos, scores(g_own), MASKED)
    m = jnp.max(s, axis=1, keepdims=True)
    p = jnp.exp2(s - m)
    l = jnp.sum(p, axis=1, keepdims=True)
    acc = weighted_values(p, g_own)

    def body(g, carry):
        m, l, acc = carry
        s = scores(g)
        m_new = jnp.maximum(m, jnp.max(s, axis=1, keepdims=True))
        alpha = jnp.exp2(m - m_new)
        p = jnp.exp2(s - m_new)
        l = alpha * l + jnp.sum(p, axis=1, keepdims=True)
        acc = per_head(alpha) * acc + weighted_values(p, g)
        return m_new, l, acc

    m, l, acc = lax.fori_loop(0, g_own, body, (m, l, acc))
    o_ref[0] = acc / per_head(l)


def _moba_prompt(proj, grp):
    b, t, _ = proj.shape
    blk = MOBA_BLOCK
    nblk = t // blk
    assert t % (grp * blk) == 0 and nblk <= HEAD_DIM
    npair = BRANCH_WIDTH // LANES
    col0 = BRANCH_WIDTH // LANES
    return pl.pallas_call(
        functools.partial(_moba_kernel, nblk=nblk, grp=grp),
        grid=(b, npair, nblk),
        in_specs=[pl.BlockSpec((1, blk, LANES), lambda bi, hp, i: (bi, i, Q_COL * col0 + hp)),
                  pl.BlockSpec((1, t, LANES), lambda bi, hp, i: (bi, 0, K_COL * col0 + hp)),
                  pl.BlockSpec((1, t, LANES), lambda bi, hp, i: (bi, 0, V_COL * col0 + hp))],
        out_specs=pl.BlockSpec((1, blk, LANES), lambda bi, hp, i: (bi, i, hp)),
        out_shape=jax.ShapeDtypeStruct((b, t, BRANCH_WIDTH), F32),
        scratch_shapes=[pltpu.VMEM((t, 2 * LANES), BF16), pltpu.VMEM((nblk // grp, 2 * grp * blk, LANES), BF16),
                        pltpu.VMEM((HEAD_DIM, LANES), F32)],
        compiler_params=_params("arbitrary", "arbitrary", "arbitrary"),
        name="moba_prompt",
    )(proj, proj, proj)


def _sb_kernel(q_ref, k_ref, v_ref, o_ref, kb_ref, vaug_ref, *, nblk, blk, grp):
    gw = grp * blk
    i = pl.program_id(2)

    @pl.when(i == 0)
    def _prepare():
        _stage_values(v_ref, vaug_ref, nblk // grp, gw)

        def body(j, c):
            r = pl.multiple_of(j * blk, blk)
            kb_ref[pl.ds(r, blk), :] = k_ref[0, pl.ds(r, blk), :].astype(BF16)
            return c

        lax.fori_loop(0, nblk, body, 0)

    lane = lax.broadcasted_iota(jnp.int32, (blk, LANES), 1)
    qs = _split_heads(q_ref[0] * (ATTN_SCALE * LOG2E), lane).astype(BF16)
    upper2 = _neg_upper2(blk)

    def tile(g, carry, q_pos):
        r = pl.multiple_of(g * gw, gw)
        z = _dot_nt(qs, kb_ref[pl.ds(r, gw), :])
        ws = [None] * grp
        for b in reversed(range(grp)):
            mask = None
            if q_pos is not None:
                k_pos = g * gw + b * blk + lax.broadcasted_iota(jnp.int32, (2 * blk, blk), 1)
                mask = k_pos < q_pos
            ws[b], carry = _stick_break_tile(z[:, b * blk:(b + 1) * blk], carry, upper2, mask)
        w = jnp.concatenate(ws, axis=1)
        both = jnp.concatenate([w[0:blk], w[blk:2 * blk]], axis=1).astype(BF16)
        return carry, _dot(both, vaug_ref[g])

    g_own = i // grp
    q_pos = i * blk + lax.broadcasted_iota(jnp.int32, (2 * blk, blk), 0) % blk
    carry, acc = tile(g_own, jnp.zeros((2 * blk, 1), F32), q_pos)

    def more(state):
        step, carry, _ = state
        return (step < g_own) & (jnp.max(carry) > UNDERFLOW_LOG2)

    def body(state):
        step, carry, acc = state
        carry, part = tile(g_own - 1 - step, carry, None)
        return step + 1, carry, acc + part

    _, _, acc = lax.while_loop(more, body, (0, carry, acc))
    o_ref[0] = acc


def _sb_prompt(proj, blk, grp):
    b, t, _ = proj.shape
    nblk = t // blk
    assert t % (grp * blk) == 0
    npair = BRANCH_WIDTH // LANES
    col0 = BRANCH_WIDTH // LANES
    return pl.pallas_call(
        functools.partial(_sb_kernel, nblk=nblk, blk=blk, grp=grp),
        grid=(b, npair, nblk),
        in_specs=[pl.BlockSpec((1, blk, LANES), lambda bi, hp, i: (bi, i, (Q_COL + 1) * col0 + hp)),
                  pl.BlockSpec((1, t, LANES), lambda bi, hp, i: (bi, 0, (K_COL + 1) * col0 + hp)),
                  pl.BlockSpec((1, t, LANES), lambda bi, hp, i: (bi, 0, (V_COL + 1) * col0 + hp))],
        out_specs=pl.BlockSpec((1, blk, LANES), lambda bi, hp, i: (bi, i, hp)),
        out_shape=jax.ShapeDtypeStruct((b, t, BRANCH_WIDTH), F32),
        scratch_shapes=[pltpu.VMEM((t, LANES), BF16), pltpu.VMEM((nblk // grp, 2 * grp * blk, LANES), BF16)],
        compiler_params=_params("arbitrary", "arbitrary", "arbitrary"),
        name="sb_prompt",
    )(proj, proj, proj)


def _pool_prompt_kernel(u_ref, halo_ref, o_ref, ext_ref, *, tm):
    i = pl.program_id(1)
    ext_ref[0:POOL_HALO, :] = jnp.where(i == 0, 0.0, halo_ref[0])
    ext_ref[POOL_HALO:POOL_HALO + tm, :] = u_ref[0]
    t = i * tm + lax.broadcasted_iota(jnp.int32, (tm, 1), 0)
    for g, w in enumerate(POOL_WINDOWS):
        cs = slice(g * POOL_GROUP_DIM, (g + 1) * POOL_GROUP_DIM)
        u = ext_ref[POOL_HALO:POOL_HALO + tm, cs]
        total = u
        for k in range(1, w):
            total = total + ext_ref[POOL_HALO - k:POOL_HALO - k + tm, cs]
        count = jnp.minimum(w, t + 1).astype(F32)
        o_ref[0, :, cs] = total / count - u


def _pool_prompt(proj, tm):
    b, t, _ = proj.shape
    assert t % tm == 0 and tm % POOL_HALO == 0
    halo_blocks = tm // POOL_HALO
    return pl.pallas_call(
        functools.partial(_pool_prompt_kernel, tm=tm),
        grid=(b, t // tm),
        in_specs=[pl.BlockSpec((1, tm, BRANCH_WIDTH), lambda bi, i: (bi, i, U_COL)),
                  pl.BlockSpec((1, POOL_HALO, BRANCH_WIDTH),
                               lambda bi, i: (bi, jnp.maximum(i * halo_blocks - 1, 0), U_COL))],
        out_specs=pl.BlockSpec((1, tm, BRANCH_WIDTH), lambda bi, i: (bi, i, 0)),
        out_shape=jax.ShapeDtypeStruct((b, t, BRANCH_WIDTH), F32),
        scratch_shapes=[pltpu.VMEM((POOL_HALO + tm, BRANCH_WIDTH), F32)],
        compiler_params=_params("arbitrary", "arbitrary"),
        name="pool_prompt",
    )(proj, proj)


def _pool_sample_kernel(state_ref, u_ref, o_ref, *, n_new, n_past):
    def ext(r):
        return state_ref[r] if r < POOL_STATE else u_ref[r - POOL_STATE]

    for t in range(n_new):
        for g, w in enumerate(POOL_WINDOWS):
            cs = slice(g * POOL_GROUP_DIM, (g + 1) * POOL_GROUP_DIM)
            total = u_ref[t][:, cs]
            for k in range(1, w):
                total = total + ext(POOL_STATE + t - k)[:, cs]
            count = float(min(w, t + 1 + n_past))
            o_ref[t, :, cs] = total / count - u_ref[t][:, cs]


def _pool_sample(state_t, u_t, n_past):
    n_new, b, c = u_t.shape
    return pl.pallas_call(
        functools.partial(_pool_sample_kernel, n_new=n_new, n_past=n_past),
        out_shape=jax.ShapeDtypeStruct((n_new, b, c), F32),
        compiler_params=pltpu.CompilerParams(vmem_limit_bytes=VMEM_LIMIT),
        name="pool_sample",
    )(state_t, u_t)


def _merge_kernel(oa_ref, ob_ref, df_ref, za_ref, zb_ref, zc_ref, ma_ref, mb_ref, mc_ref, x_ref,
                  woa_ref, wob_ref, woc_ref, wp_ref, ps_ref, wo_ref, gain_ref, bias_ref, y_ref, *, alpha):
    def silu(z):
        return z * _sigmoid(z)

    diff = df_ref[...]
    mixed = jnp.concatenate(
        [_dot(diff[:, g * POOL_GROUP_DIM:(g + 1) * POOL_GROUP_DIM].astype(BF16), wp_ref[g])
         for g in range(len(POOL_WINDOWS))], axis=1)
    o_c = mixed * ps_ref[...]
    y_a = _dot((oa_ref[...] * silu(za_ref[...])).astype(BF16), woa_ref[...])
    y_b = _dot((ob_ref[...] * silu(zb_ref[...])).astype(BF16), wob_ref[...])
    y_c = _dot((o_c * silu(zc_ref[...])).astype(BF16), woc_ref[...])
    merged = _sigmoid(ma_ref[...]) * y_a + _sigmoid(mb_ref[...]) * y_b + _sigmoid(mc_ref[...]) * y_c
    r = alpha * x_ref[...] + _dot(merged.astype(BF16), wo_ref[...])
    mu = jnp.mean(r, axis=1, keepdims=True)
    d = r - mu
    var = jnp.mean(d * d, axis=1, keepdims=True)
    y_ref[...] = d * lax.rsqrt(var + LN_EPS) * gain_ref[...] + bias_ref[...]


def _merge(o_a, o_b, diff, proj, x, wts, alpha, tm):
    rows, d_model = x.shape
    assert rows % tm == 0
    c = BRANCH_WIDTH
    gates_col0 = M_COL * c // d_model
    assert gates_col0 * d_model == M_COL * c

    def rowblk(width, colblk):
        return pl.BlockSpec((tm, width), lambda i: (i, colblk))

    def whole(a):
        return pl.BlockSpec(a.shape, lambda i: (0,) * a.ndim)

    return pl.pallas_call(
        functools.partial(_merge_kernel, alpha=alpha),
        grid=(rows // tm,),
        in_specs=[rowblk(c, 0), rowblk(c, 0), rowblk(c, 0),
                  rowblk(c, Z_COL), rowblk(c, Z_COL + 1), rowblk(c, Z_COL + 2),
                  rowblk(d_model, gates_col0), rowblk(d_model, gates_col0 + 1), rowblk(d_model, gates_col0 + 2),
                  rowblk(d_model, 0)] + [whole(a) for a in wts],
        out_specs=rowblk(d_model, 0),
        out_shape=jax.ShapeDtypeStruct((rows, d_model), F32),
        compiler_params=_params("arbitrary"),
        name="merge",
    )(o_a, o_b, diff, proj, proj, proj, proj, proj, proj, x, *wts)


def _decode_kernel(pt_ref, q_ref, kn_ref, vn_ref, *refs, n_new, nblk, page, step_blocks, layer):
    n_pages = 2 * step_blocks
    ka_refs, va_refs = refs[0:n_pages], refs[n_pages:2 * n_pages]
    kb_new_refs, vb_new_refs = refs[2 * n_pages:2 * n_pages + 2], refs[2 * n_pages + 2:2 * n_pages + 4]
    kt_hbm, vt_hbm = refs[2 * n_pages + 4:2 * n_pages + 6]
    (oa_ref, ob_ref, qb_ref, kn_pad_ref, vn_pad_ref, m_ref, l_ref, acc_ref, score_ref,
     carry_ref, accb_ref, kb_buf, vb_buf, sem) = refs[2 * n_pages + 6:]
    b = pl.program_id(0)
    s = pl.program_id(1)
    n_steps = nblk // step_blocks
    width = 2 * BRANCH_WIDTH
    nrow = HEADS_PER_BRANCH * n_new
    half = BRANCH_WIDTH

    def moba_tile(sa, values, mask, slot):
        if mask is None:
            score_ref[slot] = jnp.sum(sa, axis=1, keepdims=True) / sa.shape[1]
        else:
            sa = jnp.where(mask, sa, MASKED)
        m = jnp.max(sa, axis=1, keepdims=True)
        p = jnp.exp2(sa - m)
        m_ref[slot] = m
        l_ref[slot] = jnp.sum(p, axis=1, keepdims=True)
        acc_ref[slot] = values(p.astype(BF16))

    def stick_tile(z, values, mask):
        w, carry = _stick_break_tile(z, carry_ref[...], _neg_upper2(z.shape[1]), mask)
        carry_ref[...] = carry
        accb_ref[...] += values(w.astype(BF16))

    def live():
        return jnp.max(carry_ref[...]) > UNDERFLOW_LOG2

    @pl.when(s == 0)
    def _start():
        q = q_ref[0]
        rows = [jnp.broadcast_to(q[t:t + 1, :], (HEADS_PER_BRANCH, width)) for t in range(n_new)]
        qrep = jnp.concatenate(rows + rows, axis=0)
        r = lax.broadcasted_iota(jnp.int32, (2 * nrow, width), 0)
        c = lax.broadcasted_iota(jnp.int32, (2 * nrow, width), 1)
        head = r % HEADS_PER_BRANCH + HEADS_PER_BRANCH * (r // nrow)
        qb_ref[...] = jnp.where(c // HEAD_DIM == head, qrep * (ATTN_SCALE * LOG2E), 0.0).astype(BF16)
        kn_pad_ref[...] = jnp.zeros_like(kn_pad_ref)
        vn_pad_ref[...] = jnp.zeros_like(vn_pad_ref)
        kn_pad_ref[0:n_new, :] = kn_ref[0]
        vn_pad_ref[0:n_new, :] = vn_ref[0]
        carry_ref[...] = jnp.zeros_like(carry_ref)
        accb_ref[...] = jnp.zeros_like(accb_ref)
        token = lax.broadcasted_iota(jnp.int32, (nrow, page), 0) // HEADS_PER_BRANCH
        key = lax.broadcasted_iota(jnp.int32, (nrow, page), 1)

        def new_logits(branch):
            rows_, cols_ = slice(branch * nrow, (branch + 1) * nrow), slice(branch * half, (branch + 1) * half)
            return _dot_nt(qb_ref[rows_, cols_], kn_pad_ref[:, cols_].astype(BF16))

        def new_values(branch):
            cols_ = slice(branch * half, (branch + 1) * half)
            return lambda w: _dot(w, vn_pad_ref[:, cols_].astype(BF16))

        moba_tile(new_logits(0), new_values(0), key <= token, nblk)
        stick_tile(new_logits(1), new_values(1), key < token)

    def page_logits(branch, k0, k1):
        qh = qb_ref[branch * nrow:(branch + 1) * nrow, branch * half:(branch + 1) * half]
        return jnp.concatenate([_dot(qh, k0.astype(BF16)), _dot(qh, k1.astype(BF16))], axis=1)

    def page_values(v0, v1):
        return lambda w: (_dot_nt(w[:, 0:page], v0.astype(BF16)) + _dot_nt(w[:, page:2 * page], v1.astype(BF16)))

    def fetch_stick_half(blk):
        copies = []
        for which in range(2):
            pid = pt_ref[b, 2 * blk + which]
            copies.append(pltpu.make_async_copy(kt_hbm.at[layer, pid, pl.ds(half, half), :],
                                                kb_buf.at[which], sem.at[which]))
            copies.append(pltpu.make_async_copy(vt_hbm.at[layer, pid, pl.ds(half, half), :],
                                                vb_buf.at[which], sem.at[2 + which]))
        return copies

    for j in range(step_blocks):
        blk = nblk - 1 - (s * step_blocks + j)
        ka, va = ka_refs[2 * j:2 * j + 2], va_refs[2 * j:2 * j + 2]
        moba_tile(page_logits(0, ka[0][0, 0], ka[1][0, 0]), page_values(va[0][0, 0], va[1][0, 0]), None, blk)

        def on_demand(blk=blk):
            copies = fetch_stick_half(blk)
            for cp in copies:
                cp.start()
            for cp in copies:
                cp.wait()
            stick_tile(page_logits(1, kb_buf[0], kb_buf[1]), page_values(vb_buf[0], vb_buf[1]), None)

        if j == 0:
            @pl.when(live() & (s == 0))
            def _newest():
                stick_tile(page_logits(1, kb_new_refs[0][0, 0], kb_new_refs[1][0, 0]),
                           page_values(vb_new_refs[0][0, 0], vb_new_refs[1][0, 0]), None)

            pl.when(live() & (s > 0))(on_demand)
        else:
            pl.when(live())(on_demand)

    @pl.when(s == n_steps - 1)
    def _finish():
        sc = [score_ref[j] for j in range(nblk)]
        sel = [jnp.zeros((nrow, 1), jnp.bool_) for _ in range(nblk)]
        for _ in range(min(MOBA_TOPK, nblk)):
            mx = functools.reduce(jnp.maximum, sc)
            found = jnp.zeros((nrow, 1), jnp.bool_)
            for j in range(nblk):
                pick = (sc[j] == mx) & (sc[j] > -jnp.inf) & jnp.logical_not(found)
                sel[j] = sel[j] | pick
                found = found | pick
                sc[j] = jnp.where(pick, -jnp.inf, sc[j])
        m_all = m_ref[nblk]
        for j in range(nblk):
            m_all = jnp.maximum(m_all, jnp.where(sel[j], m_ref[j], MASKED))
        wgt = jnp.exp2(m_ref[nblk] - m_all)
        l = wgt * l_ref[nblk]
        acc = wgt * acc_ref[nblk]
        for j in range(nblk):
            wgt = jnp.where(sel[j], jnp.exp2(m_ref[j] - m_all), 0.0)
            l = l + wgt * l_ref[j]
            acc = acc + wgt * acc_ref[j]
        r = lax.broadcasted_iota(jnp.int32, (nrow, half), 0)
        c = lax.broadcasted_iota(jnp.int32, (nrow, half), 1)
        own_head = c // HEAD_DIM == r % HEADS_PER_BRANCH

        def per_token(a):
            a = jnp.where(own_head, a, 0.0).reshape(n_new, HEADS_PER_BRANCH, half)
            return jnp.sum(a, axis=1)

        oa_ref[0] = per_token(acc / l)
        ob_ref[0] = per_token(accb_ref[...])


def _decode(proj, cache_kt, cache_vt, page_table, layer):
    b, n_new, _ = proj.shape
    n_pages = page_table.shape[1]
    width, page = cache_kt.shape[2:]
    half = BRANCH_WIDTH
    assert width == 2 * half and 2 * page == MOBA_BLOCK and n_pages % 2 == 0
    assert n_new <= 8 and n_new <= page
    nblk = n_pages // 2
    step_blocks = DECODE_STEP_BLOCKS if nblk % DECODE_STEP_BLOCKS == 0 else 1
    nrow = HEADS_PER_BRANCH * n_new

    def new_spec(colblk):
        return pl.BlockSpec((1, n_new, width), lambda bi, s, pt: (bi, 0, colblk))

    def moba_half(j, which):
        return pl.BlockSpec((1, 1, half, page),
                            lambda bi, s, pt: (layer, pt[bi, 2 * (nblk - 1 - (s * step_blocks + j)) + which], 0, 0))

    def newest_stick_half(which):
        return pl.BlockSpec((1, 1, half, page), lambda bi, s, pt: (layer, pt[bi, 2 * (nblk - 1) + which], 1, 0))

    moba_specs = [moba_half(j, which) for j in range(step_blocks) for which in range(2)]
    stick_specs = [newest_stick_half(0), newest_stick_half(1)]
    hbm = pl.BlockSpec(memory_space=pl.ANY)
    out_spec = pl.BlockSpec((1, n_new, half), lambda bi, s, pt: (bi, 0, 0))
    grid_spec = pltpu.PrefetchScalarGridSpec(
        num_scalar_prefetch=1,
        grid=(b, nblk // step_blocks),
        in_specs=([new_spec(Q_COL // 2), new_spec(K_COL // 2), new_spec(V_COL // 2)] + moba_specs + moba_specs
                  + stick_specs + stick_specs + [hbm, hbm]),
        out_specs=[out_spec, out_spec],
        scratch_shapes=[pltpu.VMEM((2 * nrow, width), BF16),
                        pltpu.VMEM((page, width), F32), pltpu.VMEM((page, width), F32),
                        pltpu.VMEM((nblk + 1, nrow, 1), F32), pltpu.VMEM((nblk + 1, nrow, 1), F32),
                        pltpu.VMEM((nblk + 1, nrow, half), F32), pltpu.VMEM((nblk, nrow, 1), F32),
                        pltpu.VMEM((nrow, 1), F32), pltpu.VMEM((nrow, half), F32),
                        pltpu.VMEM((2, half, page), F32), pltpu.VMEM((2, half, page), F32),
                        pltpu.SemaphoreType.DMA((4,))],
    )
    out = jax.ShapeDtypeStruct((b, n_new, half), F32)
    n_moba = 2 * step_blocks
    return pl.pallas_call(
        functools.partial(_decode_kernel, n_new=n_new, nblk=nblk, page=page, step_blocks=step_blocks, layer=layer),
        grid_spec=grid_spec,
        out_shape=[out, out],
        compiler_params=_params("arbitrary", "arbitrary"),
        name="decode",
    )(page_table, proj, proj, proj, *([cache_kt] * n_moba), *([cache_vt] * n_moba),
      cache_kt, cache_kt, cache_vt, cache_vt, cache_kt, cache_vt)


def _row_tile(rows, want):
    return want if rows % want == 0 else rows


def _layer(xp, xs, cache_kt, cache_vt, layer, state, page_table, w_in, merge_wts, alpha, sb_block):
    b, t, d = xp.shape
    bs, ts, _ = xs.shape
    n_in = w_in.shape[1]
    c = BRANCH_WIDTH

    proj_p = _inproj(xp.reshape(b * t, d), w_in, _row_tile(b * t, 512), 1024).reshape(b, t, n_in)
    proj_s = _inproj(xs.reshape(bs * ts, d), w_in, _row_tile(bs * ts, 512), 1024).reshape(bs, ts, n_in)

    oa_p = _moba_prompt(proj_p, MOBA_GROUP)
    ob_p = _sb_prompt(proj_p, sb_block, SB_GROUP)
    diff_p = _pool_prompt(proj_p, _row_tile(t, 256))
    yp = _merge(oa_p.reshape(b * t, c), ob_p.reshape(b * t, c), diff_p.reshape(b * t, c),
                proj_p.reshape(b * t, n_in), xp.reshape(b * t, d), merge_wts, alpha,
                _row_tile(b * t, 256)).reshape(b, t, d)

    past_len = page_table.shape[1] * cache_kt.shape[3]
    oa_s, ob_s = _decode(proj_s, cache_kt, cache_vt, page_table, layer)
    state_t = state.transpose(1, 0, 2)
    u_t = proj_s[:, :, U_COL * c:(U_COL + 1) * c].transpose(1, 0, 2)
    diff_s = _pool_sample(state_t, u_t, min(POOL_STATE, past_len)).transpose(1, 0, 2)
    ys = _merge(oa_s.reshape(bs * ts, c), ob_s.reshape(bs * ts, c), diff_s.reshape(bs * ts, c),
                proj_s.reshape(bs * ts, n_in), xs.reshape(bs * ts, d), merge_wts, alpha,
                _row_tile(bs * ts, 256)).reshape(bs, ts, d)

    kv = 2 * c
    outs = dict(
        sp=proj_p[:, t - POOL_STATE:, U_COL * c:(U_COL + 1) * c],
        ks=proj_s[:, :, K_COL * c:K_COL * c + kv], vs=proj_s[:, :, V_COL * c:V_COL * c + kv],
        ss=jnp.concatenate([state_t, u_t], axis=0)[-POOL_STATE:].transpose(1, 0, 2))
    return yp, ys, outs


def kernel(x_prompt, x_sample, cache_k, cache_v, state_pool, page_table, w_in, w_out_a, w_out_b, w_out_c,
           w_pool, pool_scale, w_o, ln_gain, ln_bias):
    depth = w_in.shape[0]
    b, t, d = x_prompt.shape
    bs, ts, _ = x_sample.shape
    n_pool, page, n_heads, head_dim = cache_k.shape[1:]
    assert head_dim == HEAD_DIM and n_heads == 2 * HEADS_PER_BRANCH and t >= POOL_STATE
    alpha = (2 * depth) ** 0.25
    xp, xs = x_prompt, x_sample
    cache_kt = cache_k.transpose(0, 1, 3, 4, 2).reshape(depth, n_pool, n_heads * head_dim, page)
    cache_vt = cache_v.transpose(0, 1, 3, 4, 2).reshape(depth, n_pool, n_heads * head_dim, page)
    per_layer = []
    layer_inputs = []
    for l in range(depth):
        layer_inputs.append(xp.reshape(b * t, d))
        merge_wts = (w_out_a[l].astype(BF16), w_out_b[l].astype(BF16), w_out_c[l].astype(BF16),
                     w_pool[l].astype(BF16), pool_scale[l].reshape(1, -1), w_o[l].astype(BF16),
                     ln_gain[l].reshape(1, -1), ln_bias[l].reshape(1, -1))
        xp, xs, outs = _layer(xp, xs, cache_kt, cache_vt, l, state_pool[l],
                              page_table, w_in[l].astype(BF16), merge_wts, alpha, 256)
        per_layer.append(outs)

    def stack(name):
        return jnp.stack([o[name] for o in per_layer])

    kv = n_heads * head_dim
    wk_t = w_in[:, :, K_COL * BRANCH_WIDTH:K_COL * BRANCH_WIDTH + kv].transpose(0, 2, 1).astype(BF16)
    wv_t = w_in[:, :, V_COL * BRANCH_WIDTH:V_COL * BRANCH_WIDTH + kv].transpose(0, 2, 1).astype(BF16)
    k_pages, v_pages = _kv_pages(layer_inputs, wk_t, wv_t, page, _row_tile(b * t, 512))

    def logical(pages):
        return pages.reshape(depth, b, t // page, n_heads, head_dim, page).transpose(0, 1, 2, 5, 3, 4)

    return (xp, xs, logical(k_pages), logical(v_pages), stack("sp"),
            stack("ks").reshape(depth, bs, ts, n_heads, head_dim),
            stack("vs").reshape(depth, bs, ts, n_heads, head_dim), stack("ss"))
```

```python
import functools

import jax
import jax.numpy as jnp
from jax import lax
from jax.experimental import pallas as pl
from jax.experimental.pallas import tpu as pltpu

F32 = jnp.float32
BF16 = jnp.bfloat16

HEAD_DIM = 64
HEADS_PER_BRANCH = 8
BRANCH_WIDTH = HEADS_PER_BRANCH * HEAD_DIM
LANES = 128
MOBA_BLOCK = 256
MOBA_TOPK = 3
POOL_WINDOWS = (2, 4, 8, 16)
POOL_STATE = max(POOL_WINDOWS) - 1
POOL_HALO = 16
POOL_GROUP_DIM = BRANCH_WIDTH // len(POOL_WINDOWS)
LN_EPS = 1e-5
MASKED = -1e30
UNDERFLOW_LOG2 = -160.0
ATTN_SCALE = HEAD_DIM ** -0.5
LOG2E = 1.4426950408889634
DECODE_STEP_BLOCKS = 4
MOBA_GROUP = 4
SB_GROUP = 1
VMEM_LIMIT = 48 * 1024 * 1024

Q_COL, K_COL, V_COL, U_COL, Z_COL, M_COL = 0, 2, 4, 6, 7, 10


def _dot(a, b):
    return jnp.dot(a, b, preferred_element_type=F32)


def _dot_nt(a, b):
    return lax.dot_general(a, b, (((1,), (1,)), ((), ())), preferred_element_type=F32)


def _sigmoid(x):
    return 1.0 / (1.0 + jnp.exp(-x))


def _params(*sem):
    return pltpu.CompilerParams(dimension_semantics=sem, vmem_limit_bytes=VMEM_LIMIT)


def _inproj_kernel(x_ref, w_ref, o_ref):
    o_ref[...] = _dot(x_ref[...].astype(BF16), w_ref[...])


def _inproj(x, w, tm, tn):
    m, k = x.shape
    n = w.shape[1]
    return pl.pallas_call(
        _inproj_kernel,
        grid=(n // tn, m // tm),
        in_specs=[pl.BlockSpec((tm, k), lambda j, i: (i, 0)),
                  pl.BlockSpec((k, tn), lambda j, i: (0, j))],
        out_specs=pl.BlockSpec((tm, tn), lambda j, i: (i, j)),
        out_shape=jax.ShapeDtypeStruct((m, n), F32),
        compiler_params=_params("arbitrary", "arbitrary"),
        name="inproj",
    )(x, w)


def _kv_pages_kernel(x0_ref, x1_ref, wk_ref, wv_ref, k_ref, v_ref, *, page):
    layer = pl.program_id(0)

    def emit(x_ref):
        x = x_ref[...].astype(BF16)
        for w_ref, o_ref in ((wk_ref, k_ref), (wv_ref, v_ref)):
            t = _dot_nt(w_ref[0], x)
            for p in range(x.shape[0] // page):
                o_ref[0, p] = t[:, p * page:(p + 1) * page]

    pl.when(layer == 0)(lambda: emit(x0_ref))
    pl.when(layer != 0)(lambda: emit(x1_ref))


def _kv_pages(xs, wk_t, wv_t, page, tm):
    depth, feat, d = wk_t.shape
    rows = xs[0].shape[0]
    assert depth == len(xs) == 2 and rows % tm == 0 and tm % page == 0
    n_tiles = rows // tm

    def x_spec(which):
        return pl.BlockSpec((tm, d), lambda l, i: (jnp.where(l == which, i, (n_tiles - 1) * (1 - which)), 0))

    w_spec = pl.BlockSpec((1, feat, d), lambda l, i: (l, 0, 0))
    out_spec = pl.BlockSpec((1, tm // page, feat, page), lambda l, i: (l, i, 0, 0))
    out = jax.ShapeDtypeStruct((depth, rows // page, feat, page), F32)
    return pl.pallas_call(
        functools.partial(_kv_pages_kernel, page=page),
        grid=(depth, n_tiles),
        in_specs=[x_spec(0), x_spec(1), w_spec, w_spec],
        out_specs=[out_spec, out_spec],
        out_shape=[out, out],
        compiler_params=_params("arbitrary", "arbitrary"),
        name="kv_pages",
    )(xs[0], xs[1], wk_t, wv_t)


def _stick_break_tile(z, carry, neg_upper2, mask):
    n = jnp.maximum(z, 0.0) + jnp.log2(1.0 + jnp.exp2(-jnp.abs(z)))
    if mask is not None:
        n = jnp.where(mask, n, 0.0)
    hi = n.astype(BF16)
    lo = (n - hi.astype(F32)).astype(BF16)
    stay = _dot(jnp.concatenate([hi, lo], axis=1), neg_upper2)
    w = jnp.exp2(z + stay + carry)
    if mask is not None:
        w = jnp.where(mask, w, 0.0)
    return w, carry + stay[:, 0:1]


def _neg_upper2(n):
    j = lax.broadcasted_iota(jnp.int32, (2 * n, n), 0) % n
    s = lax.broadcasted_iota(jnp.int32, (2 * n, n), 1)
    return jnp.where(j >= s, -1.0, 0.0).astype(BF16)


def _split_heads(x, lane):
    return jnp.concatenate([jnp.where(lane < HEAD_DIM, x, 0.0), jnp.where(lane >= HEAD_DIM, x, 0.0)], axis=0)


def _stage_values(v_ref, vaug_ref, ngrp, gw):
    lane = lax.broadcasted_iota(jnp.int32, (gw, LANES), 1)

    def body(g, c):
        r = pl.multiple_of(g * gw, gw)
        vaug_ref[g] = _split_heads(v_ref[0, pl.ds(r, gw), :], lane).astype(BF16)
        return c

    lax.fori_loop(0, ngrp, body, 0)


def _moba_kernel(q_ref, qn_ref, k_ref, v_ref, o_ref, kaug_ref, vaug_ref, kmean_ref, bias_ref, *, nblk, grp):
    blk = MOBA_BLOCK
    gw = grp * blk
    i = pl.program_id(2)

    @pl.when(i == 0)
    def _prepare():
        kmean_ref[...] = jnp.zeros_like(kmean_ref)
        first_only = lax.broadcasted_iota(jnp.int32, bias_ref.shape, 1) == 0
        bias_ref[...] = jnp.where(first_only, 0.0, MASKED)
        _stage_values(v_ref, vaug_ref, nblk // grp, gw)

        def body(j, c):
            r = pl.multiple_of(j * blk, blk)
            kb = k_ref[0, pl.ds(r, blk), :]
            kmean_ref[pl.ds(j, 1), :] = jnp.sum(kb, axis=0, keepdims=True) / blk
            onehot = (lax.broadcasted_iota(jnp.int32, (blk, LANES), 1) == j).astype(F32)
            kaug_ref[pl.ds(r, blk), :] = jnp.concatenate([kb, onehot], axis=1).astype(BF16)
            return c

        lax.fori_loop(0, nblk, body, 0)

    lane = lax.broadcasted_iota(jnp.int32, (blk, HEAD_DIM), 1)
    lane_f = lane.astype(F32)
    lane2 = lax.broadcasted_iota(jnp.int32, (blk, LANES), 1)

    def selection_bias(q2, own):
        biases = []
        for h in range(2):
            qh = q2[:, h * HEAD_DIM:(h + 1) * HEAD_DIM]
            km = kmean_ref[:, h * HEAD_DIM:(h + 1) * HEAD_DIM]
            score = lax.dot_general(qh, km, (((1,), (1,)), ((), ())), precision=lax.Precision.HIGHEST,
                                    preferred_element_type=F32)
            sc = jnp.where(lane < own, score, -jnp.inf)
            selected = lane == own
            for _ in range(MOBA_TOPK):
                mx = jnp.max(sc, axis=1, keepdims=True)
                is_max = (sc == mx) & (sc > -jnp.inf)
                first = jnp.min(jnp.where(is_max, lane_f, float(HEAD_DIM)), axis=1, keepdims=True)
                pick = lane_f == first
                selected = selected | pick
                sc = jnp.where(pick, -jnp.inf, sc)
            biases.append(jnp.where(selected, 0.0, MASKED))
        return jnp.concatenate(biases, axis=0)

    bias = bias_ref[...]
    bias_next = selection_bias(qn_ref[0], i + 1)
    q_log2 = _split_heads(q_ref[0] * (ATTN_SCALE * LOG2E), lane2)
    pad = jnp.zeros((2 * blk, HEAD_DIM), F32)
    qaug = jnp.concatenate([q_log2, bias, pad], axis=1).astype(BF16)

    def scores(g):
        r = pl.multiple_of(g * gw, gw)
        return _dot_nt(qaug, kaug_ref[pl.ds(r, gw), :])

    def weighted_values(p, g):
        both = jnp.concatenate([p[0:blk], p[blk:2 * blk]], axis=1).astype(BF16)
        return _dot(both, vaug_ref[g])

    def per_head(x):
        return jnp.where(lane2 < HEAD_DIM, x[0:blk], x[blk:2 * blk])

    g_own = i // grp
    q_pos = i * blk + lax.broadcasted_iota(jnp.int32, (2 * blk, gw), 0) % blk
    k_pos = g_own * gw + lax.broadcasted_iota(jnp.int32, (2 * blk, gw), 1)
    s = jnp.where(k_pos <= q_pos, scores(g_own), MASKED)
    m = jnp.max(s, axis=1, keepdims=True)
    p = jnp.exp2(s - m)
    l = jnp.sum(p, axis=1, keepdims=True)
    acc = weighted_values(p, g_own)
    bias_ref[...] = bias_next

    def body(g, carry):
        m, l, acc = carry
        s = scores(g)
        m_new = jnp.maximum(m, jnp.max(s, axis=1, keepdims=True))
        alpha = jnp.exp2(m - m_new)
        p = jnp.exp2(s - m_new)
        l = alpha * l + jnp.sum(p, axis=1, keepdims=True)
        acc = per_head(alpha) * acc + weighted_values(p, g)
        return m_new, l, acc

    m, l, acc = lax.fori_loop(0, g_own, body, (m, l, acc))
    o_ref[0] = acc / per_head(l)


def _moba_prompt(proj, grp):
    b, t, _ = proj.shape
    blk = MOBA_BLOCK
    nblk = t // blk
    assert t % (grp * blk) == 0 and nblk <= HEAD_DIM
    npair = BRANCH_WIDTH // LANES
    col0 = BRANCH_WIDTH // LANES
    return pl.pallas_call(
        functools.partial(_moba_kernel, nblk=nblk, grp=grp),
        grid=(b, npair, nblk),
        in_specs=[pl.BlockSpec((1, blk, LANES), lambda bi, hp, i: (bi, i, Q_COL * col0 + hp)),
                  pl.BlockSpec((1, blk, LANES), lambda bi, hp, i: (bi, jnp.minimum(i + 1, nblk - 1), Q_COL * col0 + hp)),
                  pl.BlockSpec((1, t, LANES), lambda bi, hp, i: (bi, 0, K_COL * col0 + hp)),
                  pl.BlockSpec((1, t, LANES), lambda bi, hp, i: (bi, 0, V_COL * col0 + hp))],
        out_specs=pl.BlockSpec((1, blk, LANES), lambda bi, hp, i: (bi, i, hp)),
        out_shape=jax.ShapeDtypeStruct((b, t, BRANCH_WIDTH), F32),
        scratch_shapes=[pltpu.VMEM((t, 2 * LANES), BF16), pltpu.VMEM((nblk // grp, 2 * grp * blk, LANES), BF16),
                        pltpu.VMEM((HEAD_DIM, LANES), F32), pltpu.VMEM((2 * blk, HEAD_DIM), F32)],
        compiler_params=_params("arbitrary", "arbitrary", "arbitrary"),
        name="moba_prompt",
    )(proj, proj, proj, proj)


def _sb_kernel(q_ref, k_ref, v_ref, o_ref, kb_ref, vaug_ref, *, nblk, blk, grp):
    gw = grp * blk
    i = pl.program_id(2)

    @pl.when(i == 0)
    def _prepare():
        _stage_values(v_ref, vaug_ref, nblk // grp, gw)

        def body(j, c):
            r = pl.multiple_of(j * blk, blk)
            kb_ref[pl.ds(r, blk), :] = k_ref[0, pl.ds(r, blk), :].astype(BF16)
            return c

        lax.fori_loop(0, nblk, body, 0)

    lane = lax.broadcasted_iota(jnp.int32, (blk, LANES), 1)
    qs = _split_heads(q_ref[0] * (ATTN_SCALE * LOG2E), lane).astype(BF16)
    upper2 = _neg_upper2(blk)

    def tile(g, carry, q_pos):
        r = pl.multiple_of(g * gw, gw)
        z = _dot_nt(qs, kb_ref[pl.ds(r, gw), :])
        ws = [None] * grp
        for b in reversed(range(grp)):
            mask = None
            if q_pos is not None:
                k_pos = g * gw + b * blk + lax.broadcasted_iota(jnp.int32, (2 * blk, blk), 1)
                mask = k_pos < q_pos
            ws[b], carry = _stick_break_tile(z[:, b * blk:(b + 1) * blk], carry, upper2, mask)
        w = jnp.concatenate(ws, axis=1)
        both = jnp.concatenate([w[0:blk], w[blk:2 * blk]], axis=1).astype(BF16)
        return carry, _dot(both, vaug_ref[g])

    g_own = i // grp
    q_pos = i * blk + lax.broadcasted_iota(jnp.int32, (2 * blk, blk), 0) % blk
    carry, acc = tile(g_own, jnp.zeros((2 * blk, 1), F32), q_pos)

    def more(state):
        step, carry, _ = state
        return (step < g_own) & (jnp.max(carry) > UNDERFLOW_LOG2)

    def body(state):
        step, carry, acc = state
        carry, part = tile(g_own - 1 - step, carry, None)
        return step + 1, carry, acc + part

    _, _, acc = lax.while_loop(more, body, (0, carry, acc))
    o_ref[0] = acc


def _sb_prompt(proj, blk, grp):
    b, t, _ = proj.shape
    nblk = t // blk
    assert t % (grp * blk) == 0
    npair = BRANCH_WIDTH // LANES
    col0 = BRANCH_WIDTH // LANES
    return pl.pallas_call(
        functools.partial(_sb_kernel, nblk=nblk, blk=blk, grp=grp),
        grid=(b, npair, nblk),
        in_specs=[pl.BlockSpec((1, blk, LANES), lambda bi, hp, i: (bi, i, (Q_COL + 1) * col0 + hp)),
                  pl.BlockSpec((1, t, LANES), lambda bi, hp, i: (bi, 0, (K_COL + 1) * col0 + hp)),
                  pl.BlockSpec((1, t, LANES), lambda bi, hp, i: (bi, 0, (V_COL + 1) * col0 + hp))],
        out_specs=pl.BlockSpec((1, blk, LANES), lambda bi, hp, i: (bi, i, hp)),
        out_shape=jax.ShapeDtypeStruct((b, t, BRANCH_WIDTH), F32),
        scratch_shapes=[pltpu.VMEM((t, LANES), BF16), pltpu.VMEM((nblk // grp, 2 * grp * blk, LANES), BF16)],
        compiler_params=_params("arbitrary", "arbitrary", "arbitrary"),
        name="sb_prompt",
    )(proj, proj, proj)


def _pool_prompt_kernel(u_ref, halo_ref, o_ref, ext_ref, *, tm):
    i = pl.program_id(1)
    ext_ref[0:POOL_HALO, :] = jnp.where(i == 0, 0.0, halo_ref[0])
    ext_ref[POOL_HALO:POOL_HALO + tm, :] = u_ref[0]
    t = i * tm + lax.broadcasted_iota(jnp.int32, (tm, 1), 0)
    for g, w in enumerate(POOL_WINDOWS):
        cs = slice(g * POOL_GROUP_DIM, (g + 1) * POOL_GROUP_DIM)
        u = ext_ref[POOL_HALO:POOL_HALO + tm, cs]
        total = u
        for k in range(1, w):
            total = total + ext_ref[POOL_HALO - k:POOL_HALO - k + tm, cs]
        count = jnp.minimum(w, t + 1).astype(F32)
        o_ref[0, :, cs] = total / count - u


def _pool_prompt(proj, tm):
    b, t, _ = proj.shape
    assert t % tm == 0 and tm % POOL_HALO == 0
    halo_blocks = tm // POOL_HALO
    return pl.pallas_call(
        functools.partial(_pool_prompt_kernel, tm=tm),
        grid=(b, t // tm),
        in_specs=[pl.BlockSpec((1, tm, BRANCH_WIDTH), lambda bi, i: (bi, i, U_COL)),
                  pl.BlockSpec((1, POOL_HALO, BRANCH_WIDTH),
                               lambda bi, i: (bi, jnp.maximum(i * halo_blocks - 1, 0), U_COL))],
        out_specs=pl.BlockSpec((1, tm, BRANCH_WIDTH), lambda bi, i: (bi, i, 0)),
        out_shape=jax.ShapeDtypeStruct((b, t, BRANCH_WIDTH), F32),
        scratch_shapes=[pltpu.VMEM((POOL_HALO + tm, BRANCH_WIDTH), F32)],
        compiler_params=_params("arbitrary", "arbitrary"),
        name="pool_prompt",
    )(proj, proj)


def _pool_sample_kernel(state_ref, u_ref, o_ref, *, n_new, n_past):
    def ext(r):
        return state_ref[r] if r < POOL_STATE else u_ref[r - POOL_STATE]

    for t in range(n_new):
        for g, w in enumerate(POOL_WINDOWS):
            cs = slice(g * POOL_GROUP_DIM, (g + 1) * POOL_GROUP_DIM)
            total = u_ref[t][:, cs]
            for k in range(1, w):
                total = total + ext(POOL_STATE + t - k)[:, cs]
            count = float(min(w, t + 1 + n_past))
            o_ref[t, :, cs] = total / count - u_ref[t][:, cs]


def _pool_sample(state_t, u_t, n_past):
    n_new, b, c = u_t.shape
    return pl.pallas_call(
        functools.partial(_pool_sample_kernel, n_new=n_new, n_past=n_past),
        out_shape=jax.ShapeDtypeStruct((n_new, b, c), F32),
        compiler_params=pltpu.CompilerParams(vmem_limit_bytes=VMEM_LIMIT),
        name="pool_sample",
    )(state_t, u_t)


def _merge_kernel(oa_ref, ob_ref, df_ref, za_ref, zb_ref, zc_ref, ma_ref, mb_ref, mc_ref, x_ref,
                  woa_ref, wob_ref, woc_ref, wp_ref, ps_ref, wo_ref, gain_ref, bias_ref, y_ref, *, alpha):
    def silu(z):
        return z * _sigmoid(z)

    diff = df_ref[...]
    mixed = jnp.concatenate(
        [_dot(diff[:, g * POOL_GROUP_DIM:(g + 1) * POOL_GROUP_DIM].astype(BF16), wp_ref[g])
         for g in range(len(POOL_WINDOWS))], axis=1)
    o_c = mixed * ps_ref[...]
    y_a = _dot((oa_ref[...] * silu(za_ref[...])).astype(BF16), woa_ref[...])
    y_b = _dot((ob_ref[...] * silu(zb_ref[...])).astype(BF16), wob_ref[...])
    y_c = _dot((o_c * silu(zc_ref[...])).astype(BF16), woc_ref[...])
    merged = _sigmoid(ma_ref[...]) * y_a + _sigmoid(mb_ref[...]) * y_b + _sigmoid(mc_ref[...]) * y_c
    r = alpha * x_ref[...] + _dot(merged.astype(BF16), wo_ref[...])
    mu = jnp.mean(r, axis=1, keepdims=True)
    d = r - mu
    var = jnp.mean(d * d, axis=1, keepdims=True)
    y_ref[...] = d * lax.rsqrt(var + LN_EPS) * gain_ref[...] + bias_ref[...]


def _merge(o_a, o_b, diff, proj, x, wts, alpha, tm):
    rows, d_model = x.shape
    assert rows % tm == 0
    c = BRANCH_WIDTH
    gates_col0 = M_COL * c // d_model
    assert gates_col0 * d_model == M_COL * c

    def rowblk(width, colblk):
        return pl.BlockSpec((tm, width), lambda i: (i, colblk))

    def whole(a):
        return pl.BlockSpec(a.shape, lambda i: (0,) * a.ndim)

    return pl.pallas_call(
        functools.partial(_merge_kernel, alpha=alpha),
        grid=(rows // tm,),
        in_specs=[rowblk(c, 0), rowblk(c, 0), rowblk(c, 0),
                  rowblk(c, Z_COL), rowblk(c, Z_COL + 1), rowblk(c, Z_COL + 2),
                  rowblk(d_model, gates_col0), rowblk(d_model, gates_col0 + 1), rowblk(d_model, gates_col0 + 2),
                  rowblk(d_model, 0)] + [whole(a) for a in wts],
        out_specs=rowblk(d_model, 0),
        out_shape=jax.ShapeDtypeStruct((rows, d_model), F32),
        compiler_params=_params("arbitrary"),
        name="merge",
    )(o_a, o_b, diff, proj, proj, proj, proj, proj, proj, x, *wts)


def _decode_kernel(pt_ref, q_ref, kn_ref, vn_ref, *refs, n_new, nblk, page, step_blocks, layer):
    n_pages = 2 * step_blocks
    ka_refs, va_refs = refs[0:n_pages], refs[n_pages:2 * n_pages]
    kb_new_refs, vb_new_refs = refs[2 * n_pages:2 * n_pages + 2], refs[2 * n_pages + 2:2 * n_pages + 4]
    kt_hbm, vt_hbm = refs[2 * n_pages + 4:2 * n_pages + 6]
    (oa_ref, ob_ref, qb_ref, kn_pad_ref, vn_pad_ref, m_ref, l_ref, acc_ref, score_ref,
     carry_ref, accb_ref, kb_buf, vb_buf, sem) = refs[2 * n_pages + 6:]
    b = pl.program_id(0)
    s = pl.program_id(1)
    n_steps = nblk // step_blocks
    width = 2 * BRANCH_WIDTH
    nrow = HEADS_PER_BRANCH * n_new
    half = BRANCH_WIDTH

    def moba_tile(sa, values, mask, slot):
        if mask is None:
            score_ref[slot] = jnp.sum(sa, axis=1, keepdims=True) / sa.shape[1]
        else:
            sa = jnp.where(mask, sa, MASKED)
        m = jnp.max(sa, axis=1, keepdims=True)
        p = jnp.exp2(sa - m)
        m_ref[slot] = m
        l_ref[slot] = jnp.sum(p, axis=1, keepdims=True)
        acc_ref[slot] = values(p.astype(BF16))

    def stick_tile(z, values, mask):
        w, carry = _stick_break_tile(z, carry_ref[...], _neg_upper2(z.shape[1]), mask)
        carry_ref[...] = carry
        accb_ref[...] += values(w.astype(BF16))

    def live():
        return jnp.max(carry_ref[...]) > UNDERFLOW_LOG2

    @pl.when(s == 0)
    def _start():
        q = q_ref[0]
        rows = [jnp.broadcast_to(q[t:t + 1, :], (HEADS_PER_BRANCH, width)) for t in range(n_new)]
        qrep = jnp.concatenate(rows + rows, axis=0)
        r = lax.broadcasted_iota(jnp.int32, (2 * nrow, width), 0)
        c = lax.broadcasted_iota(jnp.int32, (2 * nrow, width), 1)
        head = r % HEADS_PER_BRANCH + HEADS_PER_BRANCH * (r // nrow)
        qb_ref[...] = jnp.where(c // HEAD_DIM == head, qrep * (ATTN_SCALE * LOG2E), 0.0).astype(BF16)
        kn_pad_ref[...] = jnp.zeros_like(kn_pad_ref)
        vn_pad_ref[...] = jnp.zeros_like(vn_pad_ref)
        kn_pad_ref[0:n_new, :] = kn_ref[0]
        vn_pad_ref[0:n_new, :] = vn_ref[0]
        carry_ref[...] = jnp.zeros_like(carry_ref)
        accb_ref[...] = jnp.zeros_like(accb_ref)
        token = lax.broadcasted_iota(jnp.int32, (nrow, page), 0) // HEADS_PER_BRANCH
        key = lax.broadcasted_iota(jnp.int32, (nrow, page), 1)

        def new_logits(branch):
            rows_, cols_ = slice(branch * nrow, (branch + 1) * nrow), slice(branch * half, (branch + 1) * half)
            return _dot_nt(qb_ref[rows_, cols_], kn_pad_ref[:, cols_].astype(BF16))

        def new_values(branch):
            cols_ = slice(branch * half, (branch + 1) * half)
            return lambda w: _dot(w, vn_pad_ref[:, cols_].astype(BF16))

        moba_tile(new_logits(0), new_values(0), key <= token, nblk)
        stick_tile(new_logits(1), new_values(1), key < token)

    def page_logits(branch, k0, k1):
        qh = qb_ref[branch * nrow:(branch + 1) * nrow, branch * half:(branch + 1) * half]
        return jnp.concatenate([_dot(qh, k0.astype(BF16)), _dot(qh, k1.astype(BF16))], axis=1)

    def page_values(v0, v1):
        return lambda w: (_dot_nt(w[:, 0:page], v0.astype(BF16)) + _dot_nt(w[:, page:2 * page], v1.astype(BF16)))

    def fetch_stick_half(blk):
        copies = []
        for which in range(2):
            pid = pt_ref[b, 2 * blk + which]
            copies.append(pltpu.make_async_copy(kt_hbm.at[layer, pid, pl.ds(half, half), :],
                                                kb_buf.at[which], sem.at[which]))
            copies.append(pltpu.make_async_copy(vt_hbm.at[layer, pid, pl.ds(half, half), :],
                                                vb_buf.at[which], sem.at[2 + which]))
        return copies

    for j in range(step_blocks):
        blk = nblk - 1 - (s * step_blocks + j)
        ka, va = ka_refs[2 * j:2 * j + 2], va_refs[2 * j:2 * j + 2]
        moba_tile(page_logits(0, ka[0][0, 0], ka[1][0, 0]), page_values(va[0][0, 0], va[1][0, 0]), None, blk)

    def on_demand(blk):
        copies = fetch_stick_half(blk)
        for cp in copies:
            cp.start()
        for cp in copies:
            cp.wait()
        stick_tile(page_logits(1, kb_buf[0], kb_buf[1]), page_values(vb_buf[0], vb_buf[1]), None)

    def newest():
        stick_tile(page_logits(1, kb_new_refs[0][0, 0], kb_new_refs[1][0, 0]),
                   page_values(vb_new_refs[0][0, 0], vb_new_refs[1][0, 0]), None)

    def stick_from(j):
        if j == step_blocks:
            return
        blk = nblk - 1 - (s * step_blocks + j)

        @pl.when(live())
        def _():
            if j == 0:
                pl.when(s == 0)(newest)
                pl.when(s > 0)(lambda: on_demand(blk))
            else:
                on_demand(blk)
            stick_from(j + 1)

    stick_from(0)

    @pl.when(s == n_steps - 1)
    def _finish():
        sc = [score_ref[j] for j in range(nblk)]
        sel = [jnp.zeros((nrow, 1), jnp.bool_) for _ in range(nblk)]
        for _ in range(min(MOBA_TOPK, nblk)):
            mx = functools.reduce(jnp.maximum, sc)
            found = jnp.zeros((nrow, 1), jnp.bool_)
            for j in range(nblk):
                pick = (sc[j] == mx) & (sc[j] > -jnp.inf) & jnp.logical_not(found)
                sel[j] = sel[j] | pick
                found = found | pick
                sc[j] = jnp.where(pick, -jnp.inf, sc[j])
        m_all = m_ref[nblk]
        for j in range(nblk):
            m_all = jnp.maximum(m_all, jnp.where(sel[j], m_ref[j], MASKED))
        wgt = jnp.exp2(m_ref[nblk] - m_all)
        l = wgt * l_ref[nblk]
        acc = wgt * acc_ref[nblk]
        for j in range(nblk):
            wgt = jnp.where(sel[j], jnp.exp2(m_ref[j] - m_all), 0.0)
            l = l + wgt * l_ref[j]
            acc = acc + wgt * acc_ref[j]
        r = lax.broadcasted_iota(jnp.int32, (nrow, half), 0)
        c = lax.broadcasted_iota(jnp.int32, (nrow, half), 1)
        own_head = c // HEAD_DIM == r % HEADS_PER_BRANCH

        def per_token(a):
            a = jnp.where(own_head, a, 0.0).reshape(n_new, HEADS_PER_BRANCH, half)
            return jnp.sum(a, axis=1)

        oa_ref[0] = per_token(acc / l)
        ob_ref[0] = per_token(accb_ref[...])


def _decode(proj, cache_kt, cache_vt, page_table, layer):
    b, n_new, _ = proj.shape
    n_pages = page_table.shape[1]
    width, page = cache_kt.shape[2:]
    half = BRANCH_WIDTH
    assert width == 2 * half and 2 * page == MOBA_BLOCK and n_pages % 2 == 0
    assert n_new <= 8 and n_new <= page
    nblk = n_pages // 2
    step_blocks = DECODE_STEP_BLOCKS if nblk % DECODE_STEP_BLOCKS == 0 else 1
    nrow = HEADS_PER_BRANCH * n_new

    def new_spec(colblk):
        return pl.BlockSpec((1, n_new, width), lambda bi, s, pt: (bi, 0, colblk))

    def moba_half(j, which):
        return pl.BlockSpec((1, 1, half, page),
                            lambda bi, s, pt: (layer, pt[bi, 2 * (nblk - 1 - (s * step_blocks + j)) + which], 0, 0))

    def newest_stick_half(which):
        return pl.BlockSpec((1, 1, half, page), lambda bi, s, pt: (layer, pt[bi, 2 * (nblk - 1) + which], 1, 0))

    moba_specs = [moba_half(j, which) for j in range(step_blocks) for which in range(2)]
    stick_specs = [newest_stick_half(0), newest_stick_half(1)]
    hbm = pl.BlockSpec(memory_space=pl.ANY)
    out_spec = pl.BlockSpec((1, n_new, half), lambda bi, s, pt: (bi, 0, 0))
    grid_spec = pltpu.PrefetchScalarGridSpec(
        num_scalar_prefetch=1,
        grid=(b, nblk // step_blocks),
        in_specs=([new_spec(Q_COL // 2), new_spec(K_COL // 2), new_spec(V_COL // 2)] + moba_specs + moba_specs
                  + stick_specs + stick_specs + [hbm, hbm]),
        out_specs=[out_spec, out_spec],
        scratch_shapes=[pltpu.VMEM((2 * nrow, width), BF16),
                        pltpu.VMEM((page, width), F32), pltpu.VMEM((page, width), F32),
                        pltpu.VMEM((nblk + 1, nrow, 1), F32), pltpu.VMEM((nblk + 1, nrow, 1), F32),
                        pltpu.VMEM((nblk + 1, nrow, half), F32), pltpu.VMEM((nblk, nrow, 1), F32),
                        pltpu.VMEM((nrow, 1), F32), pltpu.VMEM((nrow, half), F32),
                        pltpu.VMEM((2, half, page), F32), pltpu.VMEM((2, half, page), F32),
                        pltpu.SemaphoreType.DMA((4,))],
    )
    out = jax.ShapeDtypeStruct((b, n_new, half), F32)
    n_moba = 2 * step_blocks
    return pl.pallas_call(
        functools.partial(_decode_kernel, n_new=n_new, nblk=nblk, page=page, step_blocks=step_blocks, layer=layer),
        grid_spec=grid_spec,
        out_shape=[out, out],
        compiler_params=_params("arbitrary", "arbitrary"),
        name="decode",
    )(page_table, proj, proj, proj, *([cache_kt] * n_moba), *([cache_vt] * n_moba),
      cache_kt, cache_kt, cache_vt, cache_vt, cache_kt, cache_vt)


def _row_tile(rows, want):
    return want if rows % want == 0 else rows


def _layer(xp, xs, cache_kt, cache_vt, layer, state, page_table, w_in, merge_wts, alpha, sb_block):
    b, t, d = xp.shape
    bs, ts, _ = xs.shape
    n_in = w_in.shape[1]
    c = BRANCH_WIDTH

    proj_p = _inproj(xp.reshape(b * t, d), w_in, _row_tile(b * t, 512), 1024).reshape(b, t, n_in)
    proj_s = _inproj(xs.reshape(bs * ts, d), w_in, _row_tile(bs * ts, 512), 1024).reshape(bs, ts, n_in)

    oa_p = _moba_prompt(proj_p, MOBA_GROUP)
    ob_p = _sb_prompt(proj_p, sb_block, SB_GROUP)
    diff_p = _pool_prompt(proj_p, _row_tile(t, 256))
    yp = _merge(oa_p.reshape(b * t, c), ob_p.reshape(b * t, c), diff_p.reshape(b * t, c),
                proj_p.reshape(b * t, n_in), xp.reshape(b * t, d), merge_wts, alpha,
                _row_tile(b * t, 256)).reshape(b, t, d)

    past_len = page_table.shape[1] * cache_kt.shape[3]
    oa_s, ob_s = _decode(proj_s, cache_kt, cache_vt, page_table, layer)
    state_t = state.transpose(1, 0, 2)
    u_t = proj_s[:, :, U_COL * c:(U_COL + 1) * c].transpose(1, 0, 2)
    diff_s = _pool_sample(state_t, u_t, min(POOL_STATE, past_len)).transpose(1, 0, 2)
    ys = _merge(oa_s.reshape(bs * ts, c), ob_s.reshape(bs * ts, c), diff_s.reshape(bs * ts, c),
                proj_s.reshape(bs * ts, n_in), xs.reshape(bs * ts, d), merge_wts, alpha,
                _row_tile(bs * ts, 256)).reshape(bs, ts, d)

    kv = 2 * c
    outs = dict(
        sp=proj_p[:, t - POOL_STATE:, U_COL * c:(U_COL + 1) * c],
        ks=proj_s[:, :, K_COL * c:K_COL * c + kv], vs=proj_s[:, :, V_COL * c:V_COL * c + kv],
        ss=jnp.concatenate([state_t, u_t], axis=0)[-POOL_STATE:].transpose(1, 0, 2))
    return yp, ys, outs


def kernel(x_prompt, x_sample, cache_k, cache_v, state_pool, page_table, w_in, w_out_a, w_out_b, w_out_c,
           w_pool, pool_scale, w_o, ln_gain, ln_bias):
    depth = w_in.shape[0]
    b, t, d = x_prompt.shape
    bs, ts, _ = x_sample.shape
    n_pool, page, n_heads, head_dim = cache_k.shape[1:]
    assert head_dim == HEAD_DIM and n_heads == 2 * HEADS_PER_BRANCH and t >= POOL_STATE
    alpha = (2 * depth) ** 0.25
    xp, xs = x_prompt, x_sample
    cache_kt = cache_k.transpose(0, 1, 3, 4, 2).reshape(depth, n_pool, n_heads * head_dim, page)
    cache_vt = cache_v.transpose(0, 1, 3, 4, 2).reshape(depth, n_pool, n_heads * head_dim, page)
    per_layer = []
    layer_inputs = []
    for l in range(depth):
        layer_inputs.append(xp.reshape(b * t, d))
        merge_wts = (w_out_a[l].astype(BF16), w_out_b[l].astype(BF16), w_out_c[l].astype(BF16),
                     w_pool[l].astype(BF16), pool_scale[l].reshape(1, -1), w_o[l].astype(BF16),
                     ln_gain[l].reshape(1, -1), ln_bias[l].reshape(1, -1))
        xp, xs, outs = _layer(xp, xs, cache_kt, cache_vt, l, state_pool[l],
                              page_table, w_in[l].astype(BF16), merge_wts, alpha, 256)
        per_layer.append(outs)

    def stack(name):
        return jnp.stack([o[name] for o in per_layer])

    kv = n_heads * head_dim
    wk_t = w_in[:, :, K_COL * BRANCH_WIDTH:K_COL * BRANCH_WIDTH + kv].transpose(0, 2, 1).astype(BF16)
    wv_t = w_in[:, :, V_COL * BRANCH_WIDTH:V_COL * BRANCH_WIDTH + kv].transpose(0, 2, 1).astype(BF16)
    k_pages, v_pages = _kv_pages(layer_inputs, wk_t, wv_t, page, _row_tile(b * t, 512))

    def logical(pages):
        return pages.reshape(depth, b, t // page, n_heads, head_dim, page).transpose(0, 1, 2, 5, 3, 4)

    return (xp, xs, logical(k_pages), logical(v_pages), stack("sp"),
            stack("ks").reshape(depth, bs, ts, n_heads, head_dim),
            stack("vs").reshape(depth, bs, ts, n_heads, head_dim), stack("ss"))
```

```python
import functools

import jax
import jax.numpy as jnp
from jax import lax
from jax.experimental import pallas as pl
from jax.experimental.pallas import tpu as pltpu

F32 = jnp.float32
BF16 = jnp.bfloat16

HEAD_DIM = 64
HEADS_PER_BRANCH = 8
BRANCH_WIDTH = HEADS_PER_BRANCH * HEAD_DIM
LANES = 128
MOBA_BLOCK = 256
MOBA_TOPK = 3
POOL_WINDOWS = (2, 4, 8, 16)
POOL_STATE = max(POOL_WINDOWS) - 1
POOL_HALO = 16
POOL_GROUP_DIM = BRANCH_WIDTH // len(POOL_WINDOWS)
LN_EPS = 1e-5
MASKED = -1e30
UNDERFLOW_LOG2 = -160.0
ATTN_SCALE = HEAD_DIM ** -0.5
LOG2E = 1.4426950408889634
DECODE_STEP_BLOCKS = 4
INPROJ_COLS = 4096
MOBA_GROUP = 4
SB_GROUP = 1
VMEM_LIMIT = 48 * 1024 * 1024

Q_COL, K_COL, V_COL, U_COL, Z_COL, M_COL = 0, 2, 4, 6, 7, 10


def _dot(a, b):
    return jnp.dot(a, b, preferred_element_type=F32)


def _dot_nt(a, b):
    return lax.dot_general(a, b, (((1,), (1,)), ((), ())), preferred_element_type=F32)


def _sigmoid(x):
    return 1.0 / (1.0 + jnp.exp(-x))


def _params(*sem):
    return pltpu.CompilerParams(dimension_semantics=sem, vmem_limit_bytes=VMEM_LIMIT)


def _inproj_kernel(x_ref, w_ref, o_ref):
    o_ref[...] = _dot(x_ref[...].astype(BF16), w_ref[...])


def _inproj(x, w, tm, tn):
    m, k = x.shape
    n = w.shape[1]
    return pl.pallas_call(
        _inproj_kernel,
        grid=(n // tn, m // tm),
        in_specs=[pl.BlockSpec((tm, k), lambda j, i: (i, 0)),
                  pl.BlockSpec((k, tn), lambda j, i: (0, j))],
        out_specs=pl.BlockSpec((tm, tn), lambda j, i: (i, j)),
        out_shape=jax.ShapeDtypeStruct((m, n), F32),
        compiler_params=_params("arbitrary", "arbitrary"),
        name="inproj",
    )(x, w)


def _kv_pages_kernel(x0_ref, x1_ref, wk_ref, wv_ref, k_ref, v_ref, *, page):
    layer = pl.program_id(0)

    def emit(x_ref):
        x = x_ref[...].astype(BF16)
        for w_ref, o_ref in ((wk_ref, k_ref), (wv_ref, v_ref)):
            t = _dot_nt(w_ref[0], x)
            for p in range(x.shape[0] // page):
                o_ref[0, p] = t[:, p * page:(p + 1) * page]

    pl.when(layer == 0)(lambda: emit(x0_ref))
    pl.when(layer != 0)(lambda: emit(x1_ref))


def _kv_pages(xs, wk_t, wv_t, page, tm):
    depth, feat, d = wk_t.shape
    rows = xs[0].shape[0]
    assert depth == len(xs) == 2 and rows % tm == 0 and tm % page == 0
    n_tiles = rows // tm

    def x_spec(which):
        return pl.BlockSpec((tm, d), lambda l, i: (jnp.where(l == which, i, (n_tiles - 1) * (1 - which)), 0))

    w_spec = pl.BlockSpec((1, feat, d), lambda l, i: (l, 0, 0))
    out_spec = pl.BlockSpec((1, tm // page, feat, page), lambda l, i: (l, i, 0, 0))
    out = jax.ShapeDtypeStruct((depth, rows // page, feat, page), F32)
    return pl.pallas_call(
        functools.partial(_kv_pages_kernel, page=page),
        grid=(depth, n_tiles),
        in_specs=[x_spec(0), x_spec(1), w_spec, w_spec],
        out_specs=[out_spec, out_spec],
        out_shape=[out, out],
        compiler_params=_params("arbitrary", "arbitrary"),
        name="kv_pages",
    )(xs[0], xs[1], wk_t, wv_t)


def _stick_break_tile(z, carry, neg_upper2, mask):
    n = jnp.maximum(z, 0.0) + jnp.log2(1.0 + jnp.exp2(-jnp.abs(z)))
    if mask is not None:
        n = jnp.where(mask, n, 0.0)
    hi = n.astype(BF16)
    lo = (n - hi.astype(F32)).astype(BF16)
    stay = _dot(jnp.concatenate([hi, lo], axis=1), neg_upper2)
    w = jnp.exp2(z + stay + carry)
    if mask is not None:
        w = jnp.where(mask, w, 0.0)
    return w, carry + stay[:, 0:1]


def _neg_upper2(n):
    j = lax.broadcasted_iota(jnp.int32, (2 * n, n), 0) % n
    s = lax.broadcasted_iota(jnp.int32, (2 * n, n), 1)
    return jnp.where(j >= s, -1.0, 0.0).astype(BF16)


def _split_heads(x, lane):
    return jnp.concatenate([jnp.where(lane < HEAD_DIM, x, 0.0), jnp.where(lane >= HEAD_DIM, x, 0.0)], axis=0)


def _stage_values(v_ref, vaug_ref, ngrp, gw):
    lane = lax.broadcasted_iota(jnp.int32, (gw, LANES), 1)

    def body(g, c):
        r = pl.multiple_of(g * gw, gw)
        vaug_ref[g] = _split_heads(v_ref[0, pl.ds(r, gw), :], lane).astype(BF16)
        return c

    lax.fori_loop(0, ngrp, body, 0)


def _moba_kernel(q_ref, qn_ref, k_ref, v_ref, o_ref, kaug_ref, vaug_ref, kmean_ref, bias_ref, *, nblk, grp):
    blk = MOBA_BLOCK
    gw = grp * blk
    i = pl.program_id(2)

    @pl.when(i == 0)
    def _prepare():
        kmean_ref[...] = jnp.zeros_like(kmean_ref)
        first_only = lax.broadcasted_iota(jnp.int32, bias_ref.shape, 1) == 0
        bias_ref[...] = jnp.where(first_only, 0.0, MASKED)
        _stage_values(v_ref, vaug_ref, nblk // grp, gw)

        def body(j, c):
            r = pl.multiple_of(j * blk, blk)
            kb = k_ref[0, pl.ds(r, blk), :]
            kmean_ref[pl.ds(j, 1), :] = jnp.sum(kb, axis=0, keepdims=True) / blk
            onehot = (lax.broadcasted_iota(jnp.int32, (blk, LANES), 1) == j).astype(F32)
            kaug_ref[pl.ds(r, blk), :] = jnp.concatenate([kb, onehot], axis=1).astype(BF16)
            return c

        lax.fori_loop(0, nblk, body, 0)

    lane = lax.broadcasted_iota(jnp.int32, (blk, HEAD_DIM), 1)
    lane_f = lane.astype(F32)
    lane2 = lax.broadcasted_iota(jnp.int32, (blk, LANES), 1)

    def selection_bias(q2, own):
        biases = []
        for h in range(2):
            qh = q2[:, h * HEAD_DIM:(h + 1) * HEAD_DIM]
            km = kmean_ref[:, h * HEAD_DIM:(h + 1) * HEAD_DIM]
            score = lax.dot_general(qh, km, (((1,), (1,)), ((), ())), precision=lax.Precision.HIGHEST,
                                    preferred_element_type=F32)
            sc = jnp.where(lane < own, score, -jnp.inf)
            selected = lane == own
            for _ in range(MOBA_TOPK):
                mx = jnp.max(sc, axis=1, keepdims=True)
                is_max = (sc == mx) & (sc > -jnp.inf)
                first = jnp.min(jnp.where(is_max, lane_f, float(HEAD_DIM)), axis=1, keepdims=True)
                pick = lane_f == first
                selected = selected | pick
                sc = jnp.where(pick, -jnp.inf, sc)
            biases.append(jnp.where(selected, 0.0, MASKED))
        return jnp.concatenate(biases, axis=0)

    bias = bias_ref[...]
    bias_next = selection_bias(qn_ref[0], i + 1)
    q_log2 = _split_heads(q_ref[0] * (ATTN_SCALE * LOG2E), lane2)
    pad = jnp.zeros((2 * blk, HEAD_DIM), F32)
    qaug = jnp.concatenate([q_log2, bias, pad], axis=1).astype(BF16)

    def scores(g):
        r = pl.multiple_of(g * gw, gw)
        return _dot_nt(qaug, kaug_ref[pl.ds(r, gw), :])

    def weighted_values(p, g):
        both = jnp.concatenate([p[0:blk], p[blk:2 * blk]], axis=1).astype(BF16)
        return _dot(both, vaug_ref[g])

    def per_head(x):
        return jnp.where(lane2 < HEAD_DIM, x[0:blk], x[blk:2 * blk])

    g_own = i // grp
    q_pos = i * blk + lax.broadcasted_iota(jnp.int32, (2 * blk, gw), 0) % blk
    k_pos = g_own * gw + lax.broadcasted_iota(jnp.int32, (2 * blk, gw), 1)
    s = jnp.where(k_pos <= q_pos, scores(g_own), MASKED)
    m = jnp.max(s, axis=1, keepdims=True)
    p = jnp.exp2(s - m)
    l = jnp.sum(p, axis=1, keepdims=True)
    acc = weighted_values(p, g_own)
    bias_ref[...] = bias_next

    def body(g, carry):
        m, l, acc = carry
        s = scores(g)
        m_new = jnp.maximum(m, jnp.max(s, axis=1, keepdims=True))
        alpha = jnp.exp2(m - m_new)
        p = jnp.exp2(s - m_new)
        l = alpha * l + jnp.sum(p, axis=1, keepdims=True)
        acc = per_head(alpha) * acc + weighted_values(p, g)
        return m_new, l, acc

    m, l, acc = lax.fori_loop(0, g_own, body, (m, l, acc))
    o_ref[0] = acc / per_head(l)


def _moba_prompt(proj, grp):
    b, t, _ = proj.shape
    blk = MOBA_BLOCK
    nblk = t // blk
    assert t % (grp * blk) == 0 and nblk <= HEAD_DIM
    npair = BRANCH_WIDTH // LANES
    col0 = BRANCH_WIDTH // LANES
    return pl.pallas_call(
        functools.partial(_moba_kernel, nblk=nblk, grp=grp),
        grid=(b, npair, nblk),
        in_specs=[pl.BlockSpec((1, blk, LANES), lambda bi, hp, i: (bi, i, Q_COL * col0 + hp)),
                  pl.BlockSpec((1, blk, LANES), lambda bi, hp, i: (bi, jnp.minimum(i + 1, nblk - 1), Q_COL * col0 + hp)),
                  pl.BlockSpec((1, t, LANES), lambda bi, hp, i: (bi, 0, K_COL * col0 + hp)),
                  pl.BlockSpec((1, t, LANES), lambda bi, hp, i: (bi, 0, V_COL * col0 + hp))],
        out_specs=pl.BlockSpec((1, blk, LANES), lambda bi, hp, i: (bi, i, hp)),
        out_shape=jax.ShapeDtypeStruct((b, t, BRANCH_WIDTH), F32),
        scratch_shapes=[pltpu.VMEM((t, 2 * LANES), BF16), pltpu.VMEM((nblk // grp, 2 * grp * blk, LANES), BF16),
                        pltpu.VMEM((HEAD_DIM, LANES), F32), pltpu.VMEM((2 * blk, HEAD_DIM), F32)],
        compiler_params=_params("arbitrary", "arbitrary", "arbitrary"),
        name="moba_prompt",
    )(proj, proj, proj, proj)


def _sb_kernel(q_ref, k_ref, v_ref, o_ref, kb_ref, vaug_ref, *, nblk, blk, grp):
    gw = grp * blk
    i = pl.program_id(2)

    @pl.when(i == 0)
    def _prepare():
        _stage_values(v_ref, vaug_ref, nblk // grp, gw)

        def body(j, c):
            r = pl.multiple_of(j * blk, blk)
            kb_ref[pl.ds(r, blk), :] = k_ref[0, pl.ds(r, blk), :].astype(BF16)
            return c

        lax.fori_loop(0, nblk, body, 0)

    lane = lax.broadcasted_iota(jnp.int32, (blk, LANES), 1)
    qs = _split_heads(q_ref[0] * (ATTN_SCALE * LOG2E), lane).astype(BF16)
    upper2 = _neg_upper2(blk)

    def tile(g, carry, q_pos):
        r = pl.multiple_of(g * gw, gw)
        z = _dot_nt(qs, kb_ref[pl.ds(r, gw), :])
        ws = [None] * grp
        for b in reversed(range(grp)):
            mask = None
            if q_pos is not None:
                k_pos = g * gw + b * blk + lax.broadcasted_iota(jnp.int32, (2 * blk, blk), 1)
                mask = k_pos < q_pos
            ws[b], carry = _stick_break_tile(z[:, b * blk:(b + 1) * blk], carry, upper2, mask)
        w = jnp.concatenate(ws, axis=1)
        both = jnp.concatenate([w[0:blk], w[blk:2 * blk]], axis=1).astype(BF16)
        return carry, _dot(both, vaug_ref[g])

    g_own = i // grp
    q_pos = i * blk + lax.broadcasted_iota(jnp.int32, (2 * blk, blk), 0) % blk
    carry, acc = tile(g_own, jnp.zeros((2 * blk, 1), F32), q_pos)
    carry, part = tile(jnp.maximum(g_own - 1, 0), carry, jnp.where(g_own > 0, q_pos, -1))
    acc = acc + part

    def more(state):
        step, carry, _ = state
        return (step < g_own) & (jnp.max(carry) > UNDERFLOW_LOG2)

    def body(state):
        step, carry, acc = state
        carry, part = tile(g_own - 1 - step, carry, None)
        return step + 1, carry, acc + part

    _, _, acc = lax.while_loop(more, body, (1, carry, acc))
    o_ref[0] = acc


def _sb_prompt(proj, blk, grp):
    b, t, _ = proj.shape
    nblk = t // blk
    assert t % (grp * blk) == 0
    npair = BRANCH_WIDTH // LANES
    col0 = BRANCH_WIDTH // LANES
    return pl.pallas_call(
        functools.partial(_sb_kernel, nblk=nblk, blk=blk, grp=grp),
        grid=(b, npair, nblk),
        in_specs=[pl.BlockSpec((1, blk, LANES), lambda bi, hp, i: (bi, i, (Q_COL + 1) * col0 + hp)),
                  pl.BlockSpec((1, t, LANES), lambda bi, hp, i: (bi, 0, (K_COL + 1) * col0 + hp)),
                  pl.BlockSpec((1, t, LANES), lambda bi, hp, i: (bi, 0, (V_COL + 1) * col0 + hp))],
        out_specs=pl.BlockSpec((1, blk, LANES), lambda bi, hp, i: (bi, i, hp)),
        out_shape=jax.ShapeDtypeStruct((b, t, BRANCH_WIDTH), F32),
        scratch_shapes=[pltpu.VMEM((t, LANES), BF16), pltpu.VMEM((nblk // grp, 2 * grp * blk, LANES), BF16)],
        compiler_params=_params("arbitrary", "arbitrary", "arbitrary"),
        name="sb_prompt",
    )(proj, proj, proj)


def _pool_prompt_kernel(u_ref, halo_ref, o_ref, ext_ref, *, tm):
    i = pl.program_id(1)
    ext_ref[0:POOL_HALO, :] = jnp.where(i == 0, 0.0, halo_ref[0])
    ext_ref[POOL_HALO:POOL_HALO + tm, :] = u_ref[0]
    t = i * tm + lax.broadcasted_iota(jnp.int32, (tm, 1), 0)
    for g, w in enumerate(POOL_WINDOWS):
        cs = slice(g * POOL_GROUP_DIM, (g + 1) * POOL_GROUP_DIM)
        u = ext_ref[POOL_HALO:POOL_HALO + tm, cs]
        total = u
        for k in range(1, w):
            total = total + ext_ref[POOL_HALO - k:POOL_HALO - k + tm, cs]
        count = jnp.minimum(w, t + 1).astype(F32)
        o_ref[0, :, cs] = total / count - u


def _pool_prompt(proj, tm):
    b, t, _ = proj.shape
    assert t % tm == 0 and tm % POOL_HALO == 0
    halo_blocks = tm // POOL_HALO
    return pl.pallas_call(
        functools.partial(_pool_prompt_kernel, tm=tm),
        grid=(b, t // tm),
        in_specs=[pl.BlockSpec((1, tm, BRANCH_WIDTH), lambda bi, i: (bi, i, U_COL)),
                  pl.BlockSpec((1, POOL_HALO, BRANCH_WIDTH),
                               lambda bi, i: (bi, jnp.maximum(i * halo_blocks - 1, 0), U_COL))],
        out_specs=pl.BlockSpec((1, tm, BRANCH_WIDTH), lambda bi, i: (bi, i, 0)),
        out_shape=jax.ShapeDtypeStruct((b, t, BRANCH_WIDTH), F32),
        scratch_shapes=[pltpu.VMEM((POOL_HALO + tm, BRANCH_WIDTH), F32)],
        compiler_params=_params("arbitrary", "arbitrary"),
        name="pool_prompt",
    )(proj, proj)


def _pool_sample_kernel(state_ref, u_ref, o_ref, *, n_new, n_past):
    def ext(r):
        return state_ref[r] if r < POOL_STATE else u_ref[r - POOL_STATE]

    for t in range(n_new):
        for g, w in enumerate(POOL_WINDOWS):
            cs = slice(g * POOL_GROUP_DIM, (g + 1) * POOL_GROUP_DIM)
            total = u_ref[t][:, cs]
            for k in range(1, w):
                total = total + ext(POOL_STATE + t - k)[:, cs]
            count = float(min(w, t + 1 + n_past))
            o_ref[t, :, cs] = total / count - u_ref[t][:, cs]


def _pool_sample(state_t, u_t, n_past):
    n_new, b, c = u_t.shape
    return pl.pallas_call(
        functools.partial(_pool_sample_kernel, n_new=n_new, n_past=n_past),
        out_shape=jax.ShapeDtypeStruct((n_new, b, c), F32),
        compiler_params=pltpu.CompilerParams(vmem_limit_bytes=VMEM_LIMIT),
        name="pool_sample",
    )(state_t, u_t)


def _merge_kernel(oa_ref, ob_ref, df_ref, za_ref, zb_ref, zc_ref, ma_ref, mb_ref, mc_ref, x_ref,
                  woa_ref, wob_ref, woc_ref, wp_ref, ps_ref, wo_ref, gain_ref, bias_ref, y_ref, *, alpha):
    def silu(z):
        return z * _sigmoid(z)

    diff = df_ref[...]
    mixed = jnp.concatenate(
        [_dot(diff[:, g * POOL_GROUP_DIM:(g + 1) * POOL_GROUP_DIM].astype(BF16), wp_ref[g])
         for g in range(len(POOL_WINDOWS))], axis=1)
    o_c = mixed * ps_ref[...]
    y_a = _dot((oa_ref[...] * silu(za_ref[...])).astype(BF16), woa_ref[...])
    y_b = _dot((ob_ref[...] * silu(zb_ref[...])).astype(BF16), wob_ref[...])
    y_c = _dot((o_c * silu(zc_ref[...])).astype(BF16), woc_ref[...])
    merged = _sigmoid(ma_ref[...]) * y_a + _sigmoid(mb_ref[...]) * y_b + _sigmoid(mc_ref[...]) * y_c
    r = alpha * x_ref[...] + _dot(merged.astype(BF16), wo_ref[...])
    mu = jnp.mean(r, axis=1, keepdims=True)
    d = r - mu
    var = jnp.mean(d * d, axis=1, keepdims=True)
    y_ref[...] = d * lax.rsqrt(var + LN_EPS) * gain_ref[...] + bias_ref[...]


def _merge(o_a, o_b, diff, proj, x, wts, alpha, tm):
    rows, d_model = x.shape
    assert rows % tm == 0
    c = BRANCH_WIDTH
    gates_col0 = M_COL * c // d_model
    assert gates_col0 * d_model == M_COL * c

    def rowblk(width, colblk):
        return pl.BlockSpec((tm, width), lambda i: (i, colblk))

    def whole(a):
        return pl.BlockSpec(a.shape, lambda i: (0,) * a.ndim)

    return pl.pallas_call(
        functools.partial(_merge_kernel, alpha=alpha),
        grid=(rows // tm,),
        in_specs=[rowblk(c, 0), rowblk(c, 0), rowblk(c, 0),
                  rowblk(c, Z_COL), rowblk(c, Z_COL + 1), rowblk(c, Z_COL + 2),
                  rowblk(d_model, gates_col0), rowblk(d_model, gates_col0 + 1), rowblk(d_model, gates_col0 + 2),
                  rowblk(d_model, 0)] + [whole(a) for a in wts],
        out_specs=rowblk(d_model, 0),
        out_shape=jax.ShapeDtypeStruct((rows, d_model), F32),
        compiler_params=_params("arbitrary"),
        name="merge",
    )(o_a, o_b, diff, proj, proj, proj, proj, proj, proj, x, *wts)


def _decode_kernel(pt_ref, q_ref, kn_ref, vn_ref, *refs, n_new, nblk, page, step_blocks, layer):
    n_pages = 2 * step_blocks
    ka_refs, va_refs = refs[0:n_pages], refs[n_pages:2 * n_pages]
    kb_new_refs, vb_new_refs = refs[2 * n_pages:2 * n_pages + 2], refs[2 * n_pages + 2:2 * n_pages + 4]
    kt_hbm, vt_hbm = refs[2 * n_pages + 4:2 * n_pages + 6]
    (oa_ref, ob_ref, qb_ref, kn_pad_ref, vn_pad_ref, m_ref, l_ref, acc_ref, score_ref,
     carry_ref, accb_ref, kb_buf, vb_buf, sem) = refs[2 * n_pages + 6:]
    b = pl.program_id(0)
    s = pl.program_id(1)
    n_steps = nblk // step_blocks
    width = 2 * BRANCH_WIDTH
    nrow = HEADS_PER_BRANCH * n_new
    half = BRANCH_WIDTH

    def moba_tile(sa, values, mask, slot):
        if mask is None:
            score_ref[slot] = jnp.sum(sa, axis=1, keepdims=True) / sa.shape[1]
        else:
            sa = jnp.where(mask, sa, MASKED)
        m = jnp.max(sa, axis=1, keepdims=True)
        p = jnp.exp2(sa - m)
        m_ref[slot] = m
        l_ref[slot] = jnp.sum(p, axis=1, keepdims=True)
        acc_ref[slot] = values(p.astype(BF16))

    def stick_tile(z, values, mask):
        w, carry = _stick_break_tile(z, carry_ref[...], _neg_upper2(z.shape[1]), mask)
        carry_ref[...] = carry
        accb_ref[...] += values(w.astype(BF16))

    def live():
        return jnp.max(carry_ref[...]) > UNDERFLOW_LOG2

    @pl.when(s == 0)
    def _start():
        q = q_ref[0]
        rows = [jnp.broadcast_to(q[t:t + 1, :], (HEADS_PER_BRANCH, width)) for t in range(n_new)]
        qrep = jnp.concatenate(rows + rows, axis=0)
        r = lax.broadcasted_iota(jnp.int32, (2 * nrow, width), 0)
        c = lax.broadcasted_iota(jnp.int32, (2 * nrow, width), 1)
        head = r % HEADS_PER_BRANCH + HEADS_PER_BRANCH * (r // nrow)
        qb_ref[...] = jnp.where(c // HEAD_DIM == head, qrep * (ATTN_SCALE * LOG2E), 0.0).astype(BF16)
        kn_pad_ref[...] = jnp.zeros_like(kn_pad_ref)
        vn_pad_ref[...] = jnp.zeros_like(vn_pad_ref)
        kn_pad_ref[0:n_new, :] = kn_ref[0]
        vn_pad_ref[0:n_new, :] = vn_ref[0]
        carry_ref[...] = jnp.zeros_like(carry_ref)
        accb_ref[...] = jnp.zeros_like(accb_ref)
        token = lax.broadcasted_iota(jnp.int32, (nrow, page), 0) // HEADS_PER_BRANCH
        key = lax.broadcasted_iota(jnp.int32, (nrow, page), 1)

        def new_logits(branch):
            rows_, cols_ = slice(branch * nrow, (branch + 1) * nrow), slice(branch * half, (branch + 1) * half)
            return _dot_nt(qb_ref[rows_, cols_], kn_pad_ref[:, cols_].astype(BF16))

        def new_values(branch):
            cols_ = slice(branch * half, (branch + 1) * half)
            return lambda w: _dot(w, vn_pad_ref[:, cols_].astype(BF16))

        moba_tile(new_logits(0), new_values(0), key <= token, nblk)
        stick_tile(new_logits(1), new_values(1), key < token)

    def page_logits(branch, k0, k1):
        qh = qb_ref[branch * nrow:(branch + 1) * nrow, branch * half:(branch + 1) * half]
        return jnp.concatenate([_dot(qh, k0.astype(BF16)), _dot(qh, k1.astype(BF16))], axis=1)

    def page_values(v0, v1):
        return lambda w: (_dot_nt(w[:, 0:page], v0.astype(BF16)) + _dot_nt(w[:, page:2 * page], v1.astype(BF16)))

    def fetch_stick_half(blk):
        copies = []
        for which in range(2):
            pid = pt_ref[b, 2 * blk + which]
            copies.append(pltpu.make_async_copy(kt_hbm.at[layer, pid, pl.ds(half, half), :],
                                                kb_buf.at[which], sem.at[which]))
            copies.append(pltpu.make_async_copy(vt_hbm.at[layer, pid, pl.ds(half, half), :],
                                                vb_buf.at[which], sem.at[2 + which]))
        return copies

    for j in range(step_blocks):
        blk = nblk - 1 - (s * step_blocks + j)
        ka, va = ka_refs[2 * j:2 * j + 2], va_refs[2 * j:2 * j + 2]
        moba_tile(page_logits(0, ka[0][0, 0], ka[1][0, 0]), page_values(va[0][0, 0], va[1][0, 0]), None, blk)

    def on_demand(blk):
        copies = fetch_stick_half(blk)
        for cp in copies:
            cp.start()
        for cp in copies:
            cp.wait()
        stick_tile(page_logits(1, kb_buf[0], kb_buf[1]), page_values(vb_buf[0], vb_buf[1]), None)

    def newest():
        stick_tile(page_logits(1, kb_new_refs[0][0, 0], kb_new_refs[1][0, 0]),
                   page_values(vb_new_refs[0][0, 0], vb_new_refs[1][0, 0]), None)

    def stick_from(j):
        if j == step_blocks:
            return
        blk = nblk - 1 - (s * step_blocks + j)

        @pl.when(live())
        def _():
            if j == 0:
                pl.when(s == 0)(newest)
                pl.when(s > 0)(lambda: on_demand(blk))
            else:
                on_demand(blk)
            stick_from(j + 1)

    stick_from(0)

    @pl.when(s == n_steps - 1)
    def _finish():
        sc = [score_ref[j] for j in range(nblk)]
        sel = [jnp.zeros((nrow, 1), jnp.bool_) for _ in range(nblk)]
        for _ in range(min(MOBA_TOPK, nblk)):
            mx = functools.reduce(jnp.maximum, sc)
            found = jnp.zeros((nrow, 1), jnp.bool_)
            for j in range(nblk):
                pick = (sc[j] == mx) & (sc[j] > -jnp.inf) & jnp.logical_not(found)
                sel[j] = sel[j] | pick
                found = found | pick
                sc[j] = jnp.where(pick, -jnp.inf, sc[j])
        m_all = m_ref[nblk]
        for j in range(nblk):
            m_all = jnp.maximum(m_all, jnp.where(sel[j], m_ref[j], MASKED))
        wgt = jnp.exp2(m_ref[nblk] - m_all)
        l = wgt * l_ref[nblk]
        acc = wgt * acc_ref[nblk]
        for j in range(nblk):
            wgt = jnp.where(sel[j], jnp.exp2(m_ref[j] - m_all), 0.0)
            l = l + wgt * l_ref[j]
            acc = acc + wgt * acc_ref[j]
        r = lax.broadcasted_iota(jnp.int32, (nrow, half), 0)
        c = lax.broadcasted_iota(jnp.int32, (nrow, half), 1)
        own_head = c // HEAD_DIM == r % HEADS_PER_BRANCH

        def per_token(a):
            a = jnp.where(own_head, a, 0.0).reshape(n_new, HEADS_PER_BRANCH, half)
            return jnp.sum(a, axis=1)

        oa_ref[0] = per_token(acc / l)
        ob_ref[0] = per_token(accb_ref[...])


def _decode(proj, cache_kt, cache_vt, page_table, layer):
    b, n_new, _ = proj.shape
    n_pages = page_table.shape[1]
    width, page = cache_kt.shape[2:]
    half = BRANCH_WIDTH
    assert width == 2 * half and 2 * page == MOBA_BLOCK and n_pages % 2 == 0
    assert n_new <= 8 and n_new <= page
    nblk = n_pages // 2
    step_blocks = DECODE_STEP_BLOCKS if nblk % DECODE_STEP_BLOCKS == 0 else 1
    nrow = HEADS_PER_BRANCH * n_new

    def new_spec(colblk):
        return pl.BlockSpec((1, n_new, width), lambda bi, s, pt: (bi, 0, colblk))

    def moba_half(j, which):
        return pl.BlockSpec((1, 1, half, page),
                            lambda bi, s, pt: (layer, pt[bi, 2 * (nblk - 1 - (s * step_blocks + j)) + which], 0, 0))

    def newest_stick_half(which):
        return pl.BlockSpec((1, 1, half, page), lambda bi, s, pt: (layer, pt[bi, 2 * (nblk - 1) + which], 1, 0))

    moba_specs = [moba_half(j, which) for j in range(step_blocks) for which in range(2)]
    stick_specs = [newest_stick_half(0), newest_stick_half(1)]
    hbm = pl.BlockSpec(memory_space=pl.ANY)
    out_spec = pl.BlockSpec((1, n_new, half), lambda bi, s, pt: (bi, 0, 0))
    grid_spec = pltpu.PrefetchScalarGridSpec(
        num_scalar_prefetch=1,
        grid=(b, nblk // step_blocks),
        in_specs=([new_spec(Q_COL // 2), new_spec(K_COL // 2), new_spec(V_COL // 2)] + moba_specs + moba_specs
                  + stick_specs + stick_specs + [hbm, hbm]),
        out_specs=[out_spec, out_spec],
        scratch_shapes=[pltpu.VMEM((2 * nrow, width), BF16),
                        pltpu.VMEM((page, width), F32), pltpu.VMEM((page, width), F32),
                        pltpu.VMEM((nblk + 1, nrow, 1), F32), pltpu.VMEM((nblk + 1, nrow, 1), F32),
                        pltpu.VMEM((nblk + 1, nrow, half), F32), pltpu.VMEM((nblk, nrow, 1), F32),
                        pltpu.VMEM((nrow, 1), F32), pltpu.VMEM((nrow, half), F32),
                        pltpu.VMEM((2, half, page), F32), pltpu.VMEM((2, half, page), F32),
                        pltpu.SemaphoreType.DMA((4,))],
    )
    out = jax.ShapeDtypeStruct((b, n_new, half), F32)
    n_moba = 2 * step_blocks
    return pl.pallas_call(
        functools.partial(_decode_kernel, n_new=n_new, nblk=nblk, page=page, step_blocks=step_blocks, layer=layer),
        grid_spec=grid_spec,
        out_shape=[out, out],
        compiler_params=_params("arbitrary", "arbitrary"),
        name="decode",
    )(page_table, proj, proj, proj, *([cache_kt] * n_moba), *([cache_vt] * n_moba),
      cache_kt, cache_kt, cache_vt, cache_vt, cache_kt, cache_vt)


def _row_tile(rows, want):
    return want if rows % want == 0 else rows


def _layer(xp, xs, cache_kt, cache_vt, layer, state, page_table, w_in, merge_wts, alpha, sb_block):
    b, t, d = xp.shape
    bs, ts, _ = xs.shape
    n_in = w_in.shape[1]
    c = BRANCH_WIDTH

    proj_p = _inproj(xp.reshape(b * t, d), w_in, _row_tile(b * t, 512), INPROJ_COLS).reshape(b, t, n_in)
    proj_s = _inproj(xs.reshape(bs * ts, d), w_in, _row_tile(bs * ts, 512), INPROJ_COLS).reshape(bs, ts, n_in)

    oa_p = _moba_prompt(proj_p, MOBA_GROUP)
    ob_p = _sb_prompt(proj_p, sb_block, SB_GROUP)
    diff_p = _pool_prompt(proj_p, _row_tile(t, 256))
    yp = _merge(oa_p.reshape(b * t, c), ob_p.reshape(b * t, c), diff_p.reshape(b * t, c),
                proj_p.reshape(b * t, n_in), xp.reshape(b * t, d), merge_wts, alpha,
                _row_tile(b * t, 256)).reshape(b, t, d)

    past_len = page_table.shape[1] * cache_kt.shape[3]
    oa_s, ob_s = _decode(proj_s, cache_kt, cache_vt, page_table, layer)
    state_t = state.transpose(1, 0, 2)
    u_t = proj_s[:, :, U_COL * c:(U_COL + 1) * c].transpose(1, 0, 2)
    diff_s = _pool_sample(state_t, u_t, min(POOL_STATE, past_len)).transpose(1, 0, 2)
    ys = _merge(oa_s.reshape(bs * ts, c), ob_s.reshape(bs * ts, c), diff_s.reshape(bs * ts, c),
                proj_s.reshape(bs * ts, n_in), xs.reshape(bs * ts, d), merge_wts, alpha,
                _row_tile(bs * ts, 256)).reshape(bs, ts, d)

    kv = 2 * c
    outs = dict(
        sp=proj_p[:, t - POOL_STATE:, U_COL * c:(U_COL + 1) * c],
        ks=proj_s[:, :, K_COL * c:K_COL * c + kv], vs=proj_s[:, :, V_COL * c:V_COL * c + kv],
        ss=jnp.concatenate([state_t, u_t], axis=0)[-POOL_STATE:].transpose(1, 0, 2))
    return yp, ys, outs


def kernel(x_prompt, x_sample, cache_k, cache_v, state_pool, page_table, w_in, w_out_a, w_out_b, w_out_c,
           w_pool, pool_scale, w_o, ln_gain, ln_bias):
    depth = w_in.shape[0]
    b, t, d = x_prompt.shape
    bs, ts, _ = x_sample.shape
    n_pool, page, n_heads, head_dim = cache_k.shape[1:]
    assert head_dim == HEAD_DIM and n_heads == 2 * HEADS_PER_BRANCH and t >= POOL_STATE
    alpha = (2 * depth) ** 0.25
    xp, xs = x_prompt, x_sample
    cache_kt = cache_k.transpose(0, 1, 3, 4, 2).reshape(depth, n_pool, n_heads * head_dim, page)
    cache_vt = cache_v.transpose(0, 1, 3, 4, 2).reshape(depth, n_pool, n_heads * head_dim, page)
    per_layer = []
    layer_inputs = []
    for l in range(depth):
        layer_inputs.append(xp.reshape(b * t, d))
        merge_wts = (w_out_a[l].astype(BF16), w_out_b[l].astype(BF16), w_out_c[l].astype(BF16),
                     w_pool[l].astype(BF16), pool_scale[l].reshape(1, -1), w_o[l].astype(BF16),
                     ln_gain[l].reshape(1, -1), ln_bias[l].reshape(1, -1))
        xp, xs, outs = _layer(xp, xs, cache_kt, cache_vt, l, state_pool[l],
                              page_table, w_in[l].astype(BF16), merge_wts, alpha, 256)
        per_layer.append(outs)

    def stack(name):
        return jnp.stack([o[name] for o in per_layer])

    kv = n_heads * head_dim
    wk_t = w_in[:, :, K_COL * BRANCH_WIDTH:K_COL * BRANCH_WIDTH + kv].transpose(0, 2, 1).astype(BF16)
    wv_t = w_in[:, :, V_COL * BRANCH_WIDTH:V_COL * BRANCH_WIDTH + kv].transpose(0, 2, 1).astype(BF16)
    k_pages, v_pages = _kv_pages(layer_inputs, wk_t, wv_t, page, _row_tile(b * t, 512))

    def logical(pages):
        return pages.reshape(depth, b, t // page, n_heads, head_dim, page).transpose(0, 1, 2, 5, 3, 4)

    return (xp, xs, logical(k_pages), logical(v_pages), stack("sp"),
            stack("ks").reshape(depth, bs, ts, n_heads, head_dim),
            stack("vs").reshape(depth, bs, ts, n_heads, head_dim), stack("ss"))
```
